```python
import math
import jax, jax.numpy as jnp
from jax import lax
import numpy as np

D_MODEL = 1024
BATCH = 8
SEQ = 4096
DEPTH = 1

HEAD_DIM = 64
N_HEADS_A = 8
WIDTH_A = N_HEADS_A * HEAD_DIM
DILATED_PATTERNS = ((128, 1), (512, 4), (2048, 16))
N_HEADS_B = 4
DIFF_DIM = HEAD_DIM
DIFF_V_DIM = 2 * DIFF_DIM
WIDTH_B = N_HEADS_B * DIFF_V_DIM
MIX_WIDTH = WIDTH_A + WIDTH_B
IN_PROJ_COLS = 3 * WIDTH_A + 3 * WIDTH_B
ROPE_THETA = 500000.0
ROT_DIM = HEAD_DIM // 4
BLOCK = 128
NORM_EPS = 1e-6
ATTN_SCALE = 1.0 / math.sqrt(HEAD_DIM)

PEER_HEADS = 8
N_KEYS = 128
N_EXPERTS = N_KEYS * N_KEYS
KEY_DIM = 128
PEER_TOPK = 16
TOKEN_CHUNK = 128

kernel_name = "hybrid_dilated_diffattn_peer"


def rms_norm(x, g):
    x32 = x.astype(jnp.float32)
    y = x32 * lax.rsqrt(jnp.mean(x32 * x32, axis=-1, keepdims=True) + NORM_EPS)
    return (y * g.astype(jnp.float32)).astype(x.dtype)


def rope_tables(seq_len):
    pos = jnp.arange(seq_len, dtype=jnp.float32)
    inv_freq = ROPE_THETA ** (-jnp.arange(0, ROT_DIM, 2, dtype=jnp.float32) / ROT_DIM)
    ang = pos[:, None] * inv_freq[None, :]
    return jnp.cos(ang), jnp.sin(ang)


def apply_partial_rope(x, cos, sin):
    half = ROT_DIM // 2
    shape = (x.shape[1],) + (1,) * (x.ndim - 3) + (half,)
    c = cos.reshape(shape).astype(x.dtype)
    s = sin.reshape(shape).astype(x.dtype)
    x1 = x[..., :half]
    x2 = x[..., half:ROT_DIM]
    return jnp.concatenate([x1 * c - x2 * s, x2 * c + x1 * s, x[..., ROT_DIM:]], axis=-1)


def dilated_branch(q, k, v, window, dilation):
    B, H, S, Dh = q.shape
    w_s = window // dilation
    assert w_s <= BLOCK
    chunk = dilation * BLOCK
    L_pad = -(-S // chunk) * chunk
    pad = ((0, 0), (0, 0), (0, L_pad - S), (0, 0))
    Ls = L_pad // dilation
    nb = Ls // BLOCK

    def to_blocks(a):
        a = jnp.pad(a, pad).reshape(B, H, Ls, dilation, Dh).transpose(0, 1, 3, 2, 4)
        return a.reshape(B, H, dilation, nb, BLOCK, Dh)

    def band(a):
        prev = jnp.pad(a, ((0, 0), (0, 0), (0, 0), (1, 0), (0, 0), (0, 0)))[:, :, :, :-1]
        return jnp.concatenate([prev, a], axis=4)

    qb = to_blocks(q)
    kb = band(to_blocks(k))
    vb = band(to_blocks(v))
    s = jnp.einsum('bhrnqd,bhrnkd->bhrnqk', qb, kb,
                   preferred_element_type=jnp.float32) * ATTN_SCALE
    qi = jnp.arange(BLOCK)[:, None]
    kj = jnp.arange(2 * BLOCK)[None, :]
    dist = qi + BLOCK - kj
    blk = jnp.arange(nb)[:, None, None]
    valid = (dist >= 0) & (dist <= w_s) & (blk * BLOCK + kj - BLOCK >= 0)
    s = jnp.where(valid, s, -jnp.inf)
    m = jnp.max(s, axis=-1, keepdims=True)
    p = jnp.exp(s - m)
    denom = jnp.sum(p, axis=-1)
    o = jnp.einsum('bhrnqk,bhrnkd->bhrnqd', p, vb) / denom[..., None]
    lse = m[..., 0] + jnp.log(denom)
    o = o.reshape(B, H, dilation, Ls, Dh).transpose(0, 1, 3, 2, 4).reshape(B, H, L_pad, Dh)[:, :, :S]
    lse = lse.reshape(B, H, dilation, Ls).transpose(0, 1, 3, 2).reshape(B, H, L_pad)[:, :, :S]
    return o, lse


def dilated_mixture(q, k, v):
    outs, lses = [], []
    for window, dilation in DILATED_PATTERNS:
        o, lse = dilated_branch(q, k, v, window, dilation)
        outs.append(o)
        lses.append(lse)
    o = jnp.stack(outs)
    w = jax.nn.softmax(jnp.stack(lses), axis=0)
    return jnp.sum(w[..., None] * o, axis=0)


def diff_attention(q, k, v, lam):
    B, H, _, S, dd = q.shape
    Dv = v.shape[-1]
    nq = S // BLOCK
    qb = q.reshape(B, H, 2, nq, BLOCK, dd).transpose(3, 0, 1, 2, 4, 5)
    starts = jnp.arange(nq, dtype=jnp.int32) * BLOCK
    kpos = jnp.arange(S, dtype=jnp.int32)

    def attend(args):
        qblk, start = args
        s = jnp.einsum('bhcqd,bhckd->bhcqk', qblk, k,
                       preferred_element_type=jnp.float32) * ATTN_SCALE
        causal = (start + jnp.arange(BLOCK, dtype=jnp.int32))[:, None] >= kpos[None, :]
        p = jax.nn.softmax(jnp.where(causal, s, -jnp.inf), axis=-1)
        a = p[:, :, 0] - lam * p[:, :, 1]
        return jnp.einsum('bhqk,bhkd->bhqd', a, v)

    o = lax.map(attend, (qb, starts))
    return o.transpose(1, 0, 3, 2, 4).reshape(B, S, H, Dv)


def peer_layer(x, w_q, sub_keys, u_table, v_table):
    B, S, D = x.shape
    T = B * S
    tok = x.reshape(T, D)
    q = (tok @ w_q).reshape(T, PEER_HEADS, 2, KEY_DIM)
    s = jnp.einsum('thcd,hckd->thck', q, sub_keys, preferred_element_type=jnp.float32)
    s1, i1 = lax.top_k(s[:, :, 0], PEER_TOPK)
    s2, i2 = lax.top_k(s[:, :, 1], PEER_TOPK)
    cand = (s1[..., :, None] + s2[..., None, :]).reshape(T, PEER_HEADS, PEER_TOPK * PEER_TOPK)
    top, pos = lax.top_k(cand, PEER_TOPK)
    e1 = jnp.take_along_axis(i1, pos // PEER_TOPK, axis=-1)
    e2 = jnp.take_along_axis(i2, pos % PEER_TOPK, axis=-1)
    idx = e1 * N_KEYS + e2
    gates = jax.nn.softmax(top, axis=-1)

    n_chunks = T // TOKEN_CHUNK

    def expert_chunk(args):
        xc, ic, gc = args
        u = u_table[ic]
        act = jax.nn.gelu(jnp.einsum('cd,chkd->chk', xc, u), approximate=False) * gc.astype(xc.dtype)
        v = v_table[ic]
        return jnp.einsum('chk,chkd->cd', act, v)

    out = lax.map(expert_chunk, (tok.reshape(n_chunks, TOKEN_CHUNK, D),
                                 idx.reshape(n_chunks, TOKEN_CHUNK, PEER_HEADS, PEER_TOPK),
                                 gates.reshape(n_chunks, TOKEN_CHUNK, PEER_HEADS, PEER_TOPK)))
    return out.reshape(B, S, D)


def setup_inputs(seed: int = 0) -> dict:
    key = jax.random.key(seed)
    ks = jax.random.split(key, 19)
    f32 = jnp.float32
    L = DEPTH

    def nrm(k, shape, scale):
        return jax.random.normal(k, shape, f32) * scale

    def gain(k, shape):
        return 1.0 + 0.02 * jax.random.normal(k, shape, f32)

    return {
        "x": jax.random.normal(ks[0], (BATCH, SEQ, D_MODEL), f32),
        "attn_norm_g": gain(ks[1], (L, D_MODEL)),
        "w_in": nrm(ks[2], (L, D_MODEL, IN_PROJ_COLS), D_MODEL ** -0.5),
        "q_norm_a": gain(ks[3], (L, HEAD_DIM)),
        "k_norm_a": gain(ks[4], (L, HEAD_DIM)),
        "out_norm_a": gain(ks[5], (L, WIDTH_A)),
        "q_norm_b": gain(ks[6], (L, DIFF_DIM)),
        "k_norm_b": gain(ks[7], (L, DIFF_DIM)),
        "lambda_q1": nrm(ks[8], (L, DIFF_DIM), 0.1),
        "lambda_k1": nrm(ks[9], (L, DIFF_DIM), 0.1),
        "lambda_q2": nrm(ks[10], (L, DIFF_DIM), 0.1),
        "lambda_k2": nrm(ks[11], (L, DIFF_DIM), 0.1),
        "out_norm_b": gain(ks[12], (L, WIDTH_B)),
        "w_out": nrm(ks[13], (L, MIX_WIDTH, D_MODEL), MIX_WIDTH ** -0.5),
        "ffn_norm_g": gain(ks[14], (L, D_MODEL)),
        "w_peer_q": nrm(ks[15], (L, D_MODEL, PEER_HEADS * 2 * KEY_DIM), D_MODEL ** -0.5),
        "peer_sub_keys": nrm(ks[16], (L, PEER_HEADS, 2, N_KEYS, KEY_DIM), KEY_DIM ** -0.5),
        "peer_u": nrm(ks[17], (L, N_EXPERTS, D_MODEL), D_MODEL ** -0.5),
        "peer_v": nrm(ks[18], (L, N_EXPERTS, D_MODEL), (PEER_HEADS * PEER_TOPK) ** -0.5),
    }


def reference(x, attn_norm_g, w_in, q_norm_a, k_norm_a, out_norm_a, q_norm_b, k_norm_b,
              lambda_q1, lambda_k1, lambda_q2, lambda_k2, out_norm_b, w_out,
              ffn_norm_g, w_peer_q, peer_sub_keys, peer_u, peer_v):
    B, S, D = x.shape
    cos, sin = rope_tables(S)
    h = x
    for l in range(DEPTH):
        xn = rms_norm(h, attn_norm_g[l])
        proj = xn @ w_in[l]
        qa, ka, va, qb, kb, vb = jnp.split(
            proj, np.cumsum([WIDTH_A, WIDTH_A, WIDTH_A, WIDTH_B, WIDTH_B])[:].tolist(), axis=-1)

        qa = apply_partial_rope(rms_norm(qa.reshape(B, S, N_HEADS_A, HEAD_DIM), q_norm_a[l]), cos, sin)
        ka = apply_partial_rope(rms_norm(ka.reshape(B, S, N_HEADS_A, HEAD_DIM), k_norm_a[l]), cos, sin)
        va = va.reshape(B, S, N_HEADS_A, HEAD_DIM)
        oa = dilated_mixture(qa.transpose(0, 2, 1, 3), ka.transpose(0, 2, 1, 3), va.transpose(0, 2, 1, 3))
        oa = rms_norm(oa.transpose(0, 2, 1, 3), out_norm_a[l].reshape(N_HEADS_A, HEAD_DIM))
        oa = oa.reshape(B, S, WIDTH_A).astype(x.dtype)

        lam_init = 0.8 - 0.6 * math.exp(-0.3 * l)
        lam = (jnp.exp(jnp.sum(lambda_q1[l].astype(jnp.float32) * lambda_k1[l].astype(jnp.float32)))
               - jnp.exp(jnp.sum(lambda_q2[l].astype(jnp.float32) * lambda_k2[l].astype(jnp.float32)))
               + lam_init)
        qb = apply_partial_rope(rms_norm(qb.reshape(B, S, N_HEADS_B, 2, DIFF_DIM), q_norm_b[l]), cos, sin)
        kb = apply_partial_rope(rms_norm(kb.reshape(B, S, N_HEADS_B, 2, DIFF_DIM), k_norm_b[l]), cos, sin)
        vb = vb.reshape(B, S, N_HEADS_B, DIFF_V_DIM).transpose(0, 2, 1, 3)
        ob = diff_attention(qb.transpose(0, 2, 3, 1, 4), kb.transpose(0, 2, 3, 1, 4), vb, lam)
        ob = rms_norm(ob, out_norm_b[l].reshape(N_HEADS_B, DIFF_V_DIM)) * (1.0 - lam_init)
        ob = ob.reshape(B, S, WIDTH_B).astype(x.dtype)

        h = h + jnp.concatenate([oa, ob], axis=-1) @ w_out[l]

        hn = rms_norm(h, ffn_norm_g[l])
        h = h + peer_layer(hn, w_peer_q[l], peer_sub_keys[l], peer_u[l], peer_v[l]).astype(x.dtype)
    return h.astype(x.dtype)
```

```python
import functools
import math

import jax
import jax.numpy as jnp
from jax import lax
from jax.experimental import pallas as pl
from jax.experimental.pallas import tpu as pltpu

F32 = jnp.float32
BF16 = jnp.bfloat16

HEAD_DIM = 64
N_HEADS_A = 8
WIDTH_A = N_HEADS_A * HEAD_DIM
DILATED_PATTERNS = ((128, 1), (512, 4), (2048, 16))
N_HEADS_B = 4
DIFF_DIM = HEAD_DIM
DIFF_V_DIM = 2 * DIFF_DIM
WIDTH_B = N_HEADS_B * DIFF_V_DIM
ROPE_THETA = 500000.0
ROT_DIM = HEAD_DIM // 4
BLOCK = 128
NORM_EPS = 1e-6
ATTN_SCALE = 1.0 / math.sqrt(HEAD_DIM)
LAM_INIT = 0.8 - 0.6 * math.exp(-0.3 * 0)

PEER_HEADS = 8
N_KEYS = 128
KEY_DIM = 128
PEER_TOPK = 16
PEER_SLOTS = PEER_HEADS * PEER_TOPK

LANES = 128
VMEM_LIMIT = 56 * 1024 * 1024

NEG_INF = float("-inf")


def _cparams(sem):
    return pltpu.CompilerParams(dimension_semantics=sem, vmem_limit_bytes=VMEM_LIMIT)


def _split_bf16(a):
    hi = a.astype(BF16)
    lo = (a - hi.astype(F32)).astype(BF16)
    return hi, lo


def _group_sum(sq, bd):
    hi, lo = _split_bf16(sq)
    return (jnp.dot(hi, bd, preferred_element_type=F32)
            + jnp.dot(lo, bd, preferred_element_type=F32))


def _inproj_kernel(x_ref, g_ref, w_ref, gains_ref, bd_ref, cos_ref, sa_ref, sb_ref,
                   qa_ref, ka_ref, va_ref, qb_ref, kb_ref, vb_ref):
    x = x_ref[...]
    ms = jnp.mean(x * x, axis=-1, keepdims=True)
    xn = (x * lax.rsqrt(ms + NORM_EPS) * g_ref[...]).astype(BF16)
    proj = jnp.dot(xn, w_ref[...], preferred_element_type=F32)
    bd = bd_ref[...]
    rep = WIDTH_A // LANES
    cos = jnp.concatenate([cos_ref[...]] * rep, axis=1)
    sa = jnp.concatenate([sa_ref[...]] * rep, axis=1)
    sb = jnp.concatenate([sb_ref[...]] * rep, axis=1)
    half = ROT_DIM // 2

    def norm_rope(z, gain, scale):
        ssum = _group_sum(z * z, bd)
        zn = z * lax.rsqrt(ssum * (1.0 / HEAD_DIM) + NORM_EPS) * gain
        zr = (zn * cos + pltpu.roll(zn, WIDTH_A - half, 1) * sa + pltpu.roll(zn, half, 1) * sb)
        return zr * scale

    w = WIDTH_A
    qa_ref[...] = norm_rope(proj[:, 0:w], gains_ref[0:1, :], ATTN_SCALE).astype(BF16)
    ka_ref[...] = norm_rope(proj[:, w:2 * w], gains_ref[1:2, :], 1.0).astype(BF16)
    va_ref[...] = proj[:, 2 * w:3 * w].astype(BF16)
    qb_ref[...] = norm_rope(proj[:, 3 * w:4 * w], gains_ref[2:3, :], ATTN_SCALE).astype(BF16)
    kb_ref[...] = norm_rope(proj[:, 4 * w:5 * w], gains_ref[3:4, :], 1.0).astype(BF16)
    vb_ref[...] = proj[:, 5 * w:6 * w].astype(BF16)


def _inproj(x2, g, w_bf, gains, bd, cos_t, sa_t, sb_t, seq, tm=512):
    t, d = x2.shape
    ncol = w_bf.shape[1]
    nseq = seq // tm
    row = lambda i: (i, 0)
    fixed = lambda i: (0, 0)
    pos = lambda i: (i % nseq, 0)
    out_sds = jax.ShapeDtypeStruct((t, WIDTH_A), BF16)
    return pl.pallas_call(
        _inproj_kernel,
        grid=(t // tm,),
        in_specs=[pl.BlockSpec((tm, d), row), pl.BlockSpec((1, d), fixed),
                  pl.BlockSpec((d, ncol), fixed), pl.BlockSpec((4, WIDTH_A), fixed),
                  pl.BlockSpec((WIDTH_A, WIDTH_A), fixed),
                  pl.BlockSpec((tm, LANES), pos), pl.BlockSpec((tm, LANES), pos),
                  pl.BlockSpec((tm, LANES), pos)],
        out_specs=[pl.BlockSpec((tm, WIDTH_A), row)] * 6,
        out_shape=[out_sds] * 6,
        compiler_params=_cparams(("parallel",)),
        name="inproj",
    )(x2, g, w_bf, gains, bd, cos_t, sa_t, sb_t)


def _dilated_kernel(q_ref, kp_ref, k_ref, vp_ref, v_ref, o_ref, l_ref, kbuf, vbuf, *, tq):
    tile = pl.program_id(2)
    kbuf[0:BLOCK, :] = kp_ref[...]
    kbuf[BLOCK:, :] = k_ref[...]
    vbuf[0:BLOCK, :] = vp_ref[...]
    vbuf[BLOCK:, :] = v_ref[...]

    lane = lax.broadcasted_iota(jnp.int32, (BLOCK, LANES), 1)
    even = lane < HEAD_DIM
    head_masks = (even, lane >= HEAD_DIM)
    qi = lax.broadcasted_iota(jnp.int32, (BLOCK, 2 * BLOCK), 0)
    kj = lax.broadcasted_iota(jnp.int32, (BLOCK, 2 * BLOCK), 1)
    dist = qi + BLOCK - kj
    band = (dist >= 0) & (dist <= BLOCK)

    def body(blk, carry):
        r0 = pl.multiple_of(blk * BLOCK, BLOCK)
        first_key = jnp.where((tile == 0) & (blk == 0), BLOCK, 0)
        valid = band & (kj >= first_key)
        for pair in range(WIDTH_A // LANES):
            cols = slice(pair * LANES, (pair + 1) * LANES)
            q = q_ref[pl.ds(r0, BLOCK), cols]
            kk = kbuf[pl.ds(r0, 2 * BLOCK), cols]
            vv = vbuf[pl.ds(r0, 2 * BLOCK), cols]
            outs, lses = [], []
            for head_mask in head_masks:
                qh = jnp.where(head_mask, q, jnp.zeros_like(q))
                s = lax.dot_general(qh, kk, (((1,), (1,)), ((), ())), preferred_element_type=F32)
                s = jnp.where(valid, s, NEG_INF)
                m = jnp.max(s, axis=-1, keepdims=True)
                p = jnp.exp(s - m)
                den = jnp.sum(p, axis=-1, keepdims=True)
                o = jnp.dot(p.astype(BF16), vv, preferred_element_type=F32) / den
                outs.append(o)
                lses.append(jnp.broadcast_to(m + jnp.log(den), (BLOCK, LANES)))
            o_ref[pl.ds(r0, BLOCK), cols] = jnp.where(even, outs[0], outs[1])
            l_ref[pl.ds(r0, BLOCK), cols] = jnp.where(even, lses[0], lses[1])
        return carry

    lax.fori_loop(0, tq // BLOCK, body, 0)


def _dilated(qa, ka, va, batch, seq, dilation):
    stream = seq // dilation
    tq = min(stream, 1024)
    width = dilation * WIDTH_A
    view = lambda a: a.reshape(batch, stream, width)
    nprev = tq // BLOCK
    cur = lambda b, r, i: (b, i, r)
    prev = lambda b, r, i: (b, jnp.maximum(i * nprev - 1, 0), r)
    blk_cur = pl.BlockSpec((None, tq, WIDTH_A), cur)
    blk_prev = pl.BlockSpec((None, BLOCK, WIDTH_A), prev)
    out_sds = jax.ShapeDtypeStruct((batch, stream, width), F32)
    o, l = pl.pallas_call(
        functools.partial(_dilated_kernel, tq=tq),
        grid=(batch, dilation, stream // tq),
        in_specs=[blk_cur, blk_prev, blk_cur, blk_prev, blk_cur],
        out_specs=[blk_cur, blk_cur],
        out_shape=[out_sds, out_sds],
        scratch_shapes=[pltpu.VMEM((tq + BLOCK, WIDTH_A), BF16), pltpu.VMEM((tq + BLOCK, WIDTH_A), BF16)],
        compiler_params=_cparams(("parallel", "parallel", "arbitrary")),
        name=f"dilated_d{dilation}",
    )(view(qa), view(ka), view(ka), view(va), view(va))
    return o.reshape(batch * seq, WIDTH_A), l.reshape(batch * seq, WIDTH_A)


def _diff_kernel(lq1_ref, lk1_ref, lq2_ref, lk2_ref, gain_ref, q_ref, k_ref, v_ref, o_ref,
                 m_sc, l_sc, acc_sc, *, tq):
    i = pl.program_id(2)
    q = q_ref[...]
    lane = lax.broadcasted_iota(jnp.int32, q.shape, 1)
    zero = jnp.zeros_like(q)
    qs = (jnp.where(lane < DIFF_DIM, q, zero), jnp.where(lane >= DIFF_DIM, q, zero))

    m_sc[...] = jnp.full(m_sc.shape, NEG_INF, F32)
    l_sc[...] = jnp.zeros(l_sc.shape, F32)
    acc_sc[...] = jnp.zeros(acc_sc.shape, F32)

    row = lax.broadcasted_iota(jnp.int32, (tq, tq), 0)
    col = lax.broadcasted_iota(jnp.int32, (tq, tq), 1)
    causal = row >= col

    def step(j, masked):
        r0 = pl.multiple_of(j * tq, tq)
        kb = k_ref[pl.ds(r0, tq), :]
        vb = v_ref[pl.ds(r0, tq), :]
        for c in range(2):
            s = lax.dot_general(qs[c], kb, (((1,), (1,)), ((), ())), preferred_element_type=F32)
            if masked:
                s = jnp.where(causal, s, NEG_INF)
            m_old = m_sc[c]
            m_new = jnp.maximum(m_old, jnp.max(s, axis=-1, keepdims=True))
            alpha = jnp.exp(m_old - m_new)
            p = jnp.exp(s - m_new)
            l_sc[c] = alpha * l_sc[c] + jnp.sum(p, axis=-1, keepdims=True)
            acc_sc[c] = alpha * acc_sc[c] + jnp.dot(p.astype(BF16), vb, preferred_element_type=F32)
            m_sc[c] = m_new

    def body(j, carry):
        step(j, False)
        return carry

    lax.fori_loop(0, i, body, 0)
    step(i, True)

    lam = (jnp.exp(jnp.sum(lq1_ref[...] * lk1_ref[...], keepdims=True))
           - jnp.exp(jnp.sum(lq2_ref[...] * lk2_ref[...], keepdims=True)) + LAM_INIT)
    o = acc_sc[0] / l_sc[0] - lam * (acc_sc[1] / l_sc[1])
    ms = jnp.mean(o * o, axis=-1, keepdims=True)
    o = o * lax.rsqrt(ms + NORM_EPS) * gain_ref[...] * (1.0 - LAM_INIT)
    o_ref[...] = o.astype(o_ref.dtype)


def _diffattn(qb, kb, vb, lams, gain_b, batch, seq, tq=512):
    t = batch * seq
    nq = seq // tq
    lam_spec = pl.BlockSpec((1, DIFF_DIM), lambda b, h, i: (0, 0))
    kv_spec = pl.BlockSpec((seq, LANES), lambda b, h, i: (b, h))
    q_spec = pl.BlockSpec((tq, LANES), lambda b, h, i: (b * nq + i, h))
    return pl.pallas_call(
        functools.partial(_diff_kernel, tq=tq),
        grid=(batch, N_HEADS_B, nq),
        in_specs=[lam_spec] * 4 + [pl.BlockSpec((1, LANES), lambda b, h, i: (0, h)), q_spec, kv_spec, kv_spec],
        out_specs=q_spec,
        out_shape=jax.ShapeDtypeStruct((t, WIDTH_B), BF16),
        scratch_shapes=[pltpu.VMEM((2, tq, 1), F32), pltpu.VMEM((2, tq, 1), F32),
                        pltpu.VMEM((2, tq, DIFF_V_DIM), F32)],
        compiler_params=_cparams(("parallel", "parallel", "arbitrary")),
        name="diffattn",
    )(*lams, gain_b, qb, kb, vb)


def _outproj_kernel(x_ref, o1_ref, o2_ref, o3_ref, l1_ref, l2_ref, l3_ref, ob_ref, wa_ref, wb_ref,
                    gna_ref, bd_ref, gffn_ref, wq_ref, sk_ref, h_ref, hn_ref, sc_ref):
    l1, l2, l3 = l1_ref[...], l2_ref[...], l3_ref[...]
    lmax = jnp.maximum(jnp.maximum(l1, l2), l3)
    e1, e2, e3 = jnp.exp(l1 - lmax), jnp.exp(l2 - lmax), jnp.exp(l3 - lmax)
    oa = (e1 * o1_ref[...] + e2 * o2_ref[...] + e3 * o3_ref[...]) / (e1 + e2 + e3)
    ssum = _group_sum(oa * oa, bd_ref[...])
    oa = oa * lax.rsqrt(ssum * (1.0 / HEAD_DIM) + NORM_EPS) * gna_ref[...]
    h = (x_ref[...]
         + jnp.dot(oa.astype(BF16), wa_ref[...], preferred_element_type=F32)
         + jnp.dot(ob_ref[...], wb_ref[...], preferred_element_type=F32))
    h_ref[...] = h
    ms = jnp.mean(h * h, axis=-1, keepdims=True)
    hn = h * lax.rsqrt(ms + NORM_EPS) * gffn_ref[...]
    hn_ref[...] = hn
    q = jnp.dot(hn.astype(BF16), wq_ref[...], preferred_element_type=F32).astype(BF16)
    for hc in range(2 * PEER_HEADS):
        qs = q[:, hc * KEY_DIM:(hc + 1) * KEY_DIM]
        sc_ref[hc] = lax.dot_general(sk_ref[hc], qs, (((1,), (1,)), ((), ())), preferred_element_type=F32)


def _outproj(x2, o_l, ob, wa, wb, gna, bd, gffn, wq, sk, tm=256):
    t, d = x2.shape
    row = lambda i: (i, 0)
    fixed = lambda i: (0, 0)
    half = pl.BlockSpec((tm, WIDTH_A), row)
    full = pl.BlockSpec((tm, d), row)
    nq = wq.shape[1]
    return pl.pallas_call(
        _outproj_kernel,
        grid=(t // tm,),
        in_specs=[full] + [half] * 7
                 + [pl.BlockSpec((WIDTH_A, d), fixed), pl.BlockSpec((WIDTH_B, d), fixed),
                    pl.BlockSpec((1, WIDTH_A), fixed), pl.BlockSpec((WIDTH_A, WIDTH_A), fixed),
                    pl.BlockSpec((1, d), fixed), pl.BlockSpec((d, nq), fixed),
                    pl.BlockSpec((2 * PEER_HEADS, N_KEYS, KEY_DIM), lambda i: (0, 0, 0))],
        out_specs=[full, full, pl.BlockSpec((2 * PEER_HEADS, N_KEYS, tm), lambda i: (0, 0, i))],
        out_shape=[jax.ShapeDtypeStruct((t, d), F32), jax.ShapeDtypeStruct((t, d), F32),
                   jax.ShapeDtypeStruct((2 * PEER_HEADS, N_KEYS, t), F32)],
        compiler_params=_cparams(("parallel",)),
        name="outproj",
    )(x2, *o_l, ob, wa, wb, gna, bd, gffn, wq, sk)


def _extract_top(vals, payload, count):
    rows = vals.shape[0]
    iota = lax.broadcasted_iota(jnp.int32, vals.shape, 0)
    tops, picks = [], []
    for _ in range(count):
        m = jnp.max(vals, axis=0, keepdims=True)
        win = jnp.min(jnp.where(vals == m, iota, rows), axis=0, keepdims=True)
        hit = iota == win
        tops.append(m)
        picks.append(win if payload is None else jnp.sum(jnp.where(hit, payload, 0), axis=0, keepdims=True))
        vals = jnp.where(hit, NEG_INF, vals)
    return tops, picks


def _topk_kernel(sc_ref, idx_ref, gate_ref):
    for h in range(PEER_HEADS):
        v1, i1 = _extract_top(sc_ref[2 * h], None, PEER_TOPK)
        v2, i2 = _extract_top(sc_ref[2 * h + 1], None, PEER_TOPK)
        v2m = jnp.concatenate(v2, axis=0)
        i2m = jnp.concatenate(i2, axis=0)
        cand = jnp.concatenate([v1[a] + v2m for a in range(PEER_TOPK)], axis=0)
        cand_id = jnp.concatenate([i1[a] * N_KEYS + i2m for a in range(PEER_TOPK)], axis=0)
        top, eid = _extract_top(cand, cand_id, PEER_TOPK)
        ex = [jnp.exp(tv - top[0]) for tv in top]
        den = ex[0]
        for e in ex[1:]:
            den = den + e
        for k in range(PEER_TOPK):
            j = h * PEER_TOPK + k
            idx_ref[0, j:j + 1, :] = eid[k]
            gate_ref[0, j:j + 1, :] = ex[k] / den


def _topk(scores_t, tt):
    t = scores_t.shape[-1]
    nt = t // tt
    out_spec = pl.BlockSpec((1, PEER_SLOTS, tt), lambda i: (i, 0, 0))
    return pl.pallas_call(
        _topk_kernel,
        grid=(nt,),
        in_specs=[pl.BlockSpec((2 * PEER_HEADS, N_KEYS, tt), lambda i: (0, 0, i))],
        out_specs=[out_spec, out_spec],
        out_shape=[jax.ShapeDtypeStruct((nt, PEER_SLOTS, tt), jnp.int32),
                   jax.ShapeDtypeStruct((nt, PEER_SLOTS, tt), F32)],
        compiler_params=_cparams(("parallel",)),
        name="topk",
    )(scores_t)


ROW_WORDS = 4


def _unpack(words):
    lo = pltpu.bitcast(words << 16, F32)
    hi = pltpu.bitcast(words & jnp.int32(-65536), F32)
    return lo, hi


def _peer_dot_kernel(idx_ref, x_ref, tab_ref, gate_ref, act_ref, part_sc, acc_sc, *, tt):
    lane = lax.broadcasted_iota(jnp.int32, (PEER_SLOTS, tt), 1)
    acc_sc[...] = jnp.zeros(acc_sc.shape, F32)

    def token(t, carry):
        xt = x_ref[t]
        xlo, xhi = xt[0:ROW_WORDS], xt[ROW_WORDS:]
        for j in range(PEER_SLOTS):
            e = idx_ref[0, j, t]
            lo, hi = _unpack(tab_ref[pl.ds(pl.multiple_of(e * ROW_WORDS, ROW_WORDS), ROW_WORDS), :])
            part_sc[j * ROW_WORDS:(j + 1) * ROW_WORDS, :] = lo * xlo + hi * xhi
        red = part_sc[pl.ds(0, PEER_SLOTS, stride=ROW_WORDS), :]
        for r in range(1, ROW_WORDS):
            red = red + part_sc[pl.ds(r, PEER_SLOTS, stride=ROW_WORDS), :]
        col = jnp.sum(red, axis=-1, keepdims=True)
        acc_sc[...] = jnp.where(lane == t, col, acc_sc[...])
        return carry

    lax.fori_loop(0, tt, token, 0)
    z = acc_sc[...]
    gelu = 0.5 * z * (1.0 + lax.erf(z * (1.0 / math.sqrt(2.0))))
    act_ref[0] = gelu * gate_ref[0]


def _peer_dot(idx_t, hn3, tab, gate_t, tt):
    nt = idx_t.shape[0]
    tile = lambda i: (i, 0, 0)
    return pl.pallas_call(
        functools.partial(_peer_dot_kernel, tt=tt),
        grid=(nt,),
        in_specs=[pl.BlockSpec((1, PEER_SLOTS, tt), tile, memory_space=pltpu.SMEM),
                  pl.BlockSpec((tt, 8, LANES), tile),
                  pl.BlockSpec(memory_space=pltpu.VMEM),
                  pl.BlockSpec((1, PEER_SLOTS, tt), tile)],
        out_specs=pl.BlockSpec((1, PEER_SLOTS, tt), tile),
        out_shape=jax.ShapeDtypeStruct((nt, PEER_SLOTS, tt), F32),
        scratch_shapes=[pltpu.VMEM((PEER_SLOTS * ROW_WORDS, LANES), F32), pltpu.VMEM((PEER_SLOTS, tt), F32)],
        compiler_params=_cparams(("arbitrary",)),
        name="peer_dot",
    )(idx_t, hn3, tab, gate_t)


def _peer_out_kernel(idx_ref, act_ref, tab_ref, h_ref, y_ref, *, tt):
    nacc = 4

    def token(t, carry):
        acc = [None] * (2 * nacc)
        for j in range(PEER_SLOTS):
            e = idx_ref[0, j, t]
            a = act_ref[0, j, t]
            lo, hi = _unpack(tab_ref[pl.ds(pl.multiple_of(e * ROW_WORDS, ROW_WORDS), ROW_WORDS), :])
            k = 2 * (j % nacc)
            acc[k] = a * lo if acc[k] is None else acc[k] + a * lo
            acc[k + 1] = a * hi if acc[k + 1] is None else acc[k + 1] + a * hi
        lo = (acc[0] + acc[2]) + (acc[4] + acc[6])
        hi = (acc[1] + acc[3]) + (acc[5] + acc[7])
        ht = h_ref[t]
        y_ref[t, 0:ROW_WORDS, :] = ht[0:ROW_WORDS] + lo
        y_ref[t, ROW_WORDS:, :] = ht[ROW_WORDS:] + hi
        return carry

    lax.fori_loop(0, tt, token, 0)


def _peer_out(idx_t, act_t, tab, h3, tt):
    nt = idx_t.shape[0]
    tile = lambda i: (i, 0, 0)
    smem = pl.BlockSpec((1, PEER_SLOTS, tt), tile, memory_space=pltpu.SMEM)
    tok = pl.BlockSpec((tt, 8, LANES), tile)
    return pl.pallas_call(
        functools.partial(_peer_out_kernel, tt=tt),
        grid=(nt,),
        in_specs=[smem, smem, pl.BlockSpec(memory_space=pltpu.VMEM), tok],
        out_specs=tok,
        out_shape=jax.ShapeDtypeStruct(h3.shape, F32),
        compiler_params=_cparams(("arbitrary",)),
        name="peer_out",
    )(idx_t, act_t, tab, h3)


def _pack_table(tab):
    n, d = tab.shape
    bits = lax.bitcast_convert_type(tab.astype(BF16), jnp.uint16).astype(jnp.uint32)
    words = bits[:, :d // 2] | (bits[:, d // 2:] << 16)
    return lax.bitcast_convert_type(words, jnp.int32).reshape(n * ROW_WORDS, LANES)


def _rope_tables(seq):
    pos = jnp.arange(seq, dtype=F32)
    inv_freq = ROPE_THETA ** (-jnp.arange(0, ROT_DIM, 2, dtype=F32) / ROT_DIM)
    ang = pos[:, None] * inv_freq[None, :]
    cos, sin = jnp.cos(ang), jnp.sin(ang)
    half = ROT_DIM // 2
    pad1 = jnp.ones((seq, HEAD_DIM - ROT_DIM), F32)
    pad0 = jnp.zeros((seq, HEAD_DIM - ROT_DIM), F32)
    zeros = jnp.zeros((seq, half), F32)
    cos_h = jnp.concatenate([cos, cos, pad1], axis=1)
    sa_h = jnp.concatenate([-sin, zeros, pad0], axis=1)
    sb_h = jnp.concatenate([zeros, sin, pad0], axis=1)
    two = lambda a: jnp.concatenate([a, a], axis=1)
    return two(cos_h), two(sa_h), two(sb_h)


def _layer(x2, batch, seq, attn_norm_g, w_in, q_norm_a, k_norm_a, out_norm_a, q_norm_b, k_norm_b,
           lambda_q1, lambda_k1, lambda_q2, lambda_k2, out_norm_b, w_out, ffn_norm_g, w_peer_q,
           peer_sub_keys, peer_u, peer_v, peer_tt=128):
    d = x2.shape[1]
    tile_heads = lambda g: jnp.tile(g.astype(F32), WIDTH_A // HEAD_DIM)[None, :]
    gains = jnp.concatenate([tile_heads(q_norm_a), tile_heads(k_norm_a),
                             tile_heads(q_norm_b), tile_heads(k_norm_b)], axis=0)
    grp = jnp.arange(WIDTH_A) // HEAD_DIM
    bd = (grp[:, None] == grp[None, :]).astype(BF16)
    cos_t, sa_t, sb_t = _rope_tables(seq)

    qa, ka, va, qb, kb, vb = _inproj(x2, attn_norm_g[None, :], w_in.astype(BF16), gains, bd,
                                     cos_t, sa_t, sb_t, seq)
    outs = [_dilated(qa, ka, va, batch, seq, dil) for _, dil in DILATED_PATTERNS]
    lams = [v[None, :].astype(F32) for v in (lambda_q1, lambda_k1, lambda_q2, lambda_k2)]
    ob = _diffattn(qb, kb, vb, lams, out_norm_b[None, :].astype(F32), batch, seq)

    w_out_bf = w_out.astype(BF16)
    sk = peer_sub_keys.reshape(2 * PEER_HEADS, N_KEYS, KEY_DIM).astype(BF16)
    h, hn, scores_t = _outproj(x2, [o for o, _ in outs] + [l for _, l in outs], ob,
                               w_out_bf[:WIDTH_A], w_out_bf[WIDTH_A:], out_norm_a[None, :].astype(F32), bd,
                               ffn_norm_g[None, :].astype(F32), w_peer_q.astype(BF16), sk)
    idx_t, gate_t = _topk(scores_t, peer_tt)
    t = x2.shape[0]
    act_t = _peer_dot(idx_t, hn.reshape(t, d // LANES, LANES), _pack_table(peer_u), gate_t, peer_tt)
    y3 = _peer_out(idx_t, act_t, _pack_table(peer_v), h.reshape(t, d // LANES, LANES), peer_tt)
    return y3.reshape(t, d)


def kernel(x, attn_norm_g, w_in, q_norm_a, k_norm_a, out_norm_a, q_norm_b, k_norm_b, lambda_q1, lambda_k1,
           lambda_q2, lambda_k2, out_norm_b, w_out, ffn_norm_g, w_peer_q, peer_sub_keys, peer_u, peer_v):
    batch, seq, d = x.shape
    y = _layer(x.reshape(batch * seq, d), batch, seq, attn_norm_g[0], w_in[0], q_norm_a[0], k_norm_a[0],
               out_norm_a[0], q_norm_b[0], k_norm_b[0], lambda_q1[0], lambda_k1[0], lambda_q2[0],
               lambda_k2[0], out_norm_b[0], w_out[0], ffn_norm_g[0], w_peer_q[0], peer_sub_keys[0],
               peer_u[0], peer_v[0])
    return y.reshape(batch, seq, d)
```

```python
import functools
import math

import jax
import jax.numpy as jnp
from jax import lax
from jax.experimental import pallas as pl
from jax.experimental.pallas import tpu as pltpu

F32 = jnp.float32
BF16 = jnp.bfloat16

HEAD_DIM = 64
N_HEADS_A = 8
WIDTH_A = N_HEADS_A * HEAD_DIM
DILATED_PATTERNS = ((128, 1), (512, 4), (2048, 16))
N_HEADS_B = 4
DIFF_DIM = HEAD_DIM
DIFF_V_DIM = 2 * DIFF_DIM
WIDTH_B = N_HEADS_B * DIFF_V_DIM
ROPE_THETA = 500000.0
ROT_DIM = HEAD_DIM // 4
BLOCK = 128
NORM_EPS = 1e-6
ATTN_SCALE = 1.0 / math.sqrt(HEAD_DIM)
LAM_INIT = 0.8 - 0.6 * math.exp(-0.3 * 0)

PEER_HEADS = 8
N_KEYS = 128
KEY_DIM = 128
PEER_TOPK = 16
PEER_SLOTS = PEER_HEADS * PEER_TOPK
ROW_WORDS = 4
ROW_CHUNKS = 2 * ROW_WORDS
PEER_GROUP = 4

LANES = 128
VMEM_LIMIT = 56 * 1024 * 1024

NEG_INF = float("-inf")


def _cparams(sem):
    return pltpu.CompilerParams(dimension_semantics=sem, vmem_limit_bytes=VMEM_LIMIT)


def _split_bf16(a):
    hi = a.astype(BF16)
    lo = (a - hi.astype(F32)).astype(BF16)
    return hi, lo


def _group_sum(sq, bd):
    hi, lo = _split_bf16(sq)
    return (jnp.dot(hi, bd, preferred_element_type=F32)
            + jnp.dot(lo, bd, preferred_element_type=F32))


def _inproj_kernel(x_ref, g_ref, w_ref, gains_ref, bd_ref, cos_ref, sa_ref, sb_ref,
                   qa_ref, ka_ref, va_ref, qb_ref, kb_ref, vb_ref):
    x = x_ref[...]
    ms = jnp.mean(x * x, axis=-1, keepdims=True)
    xn = (x * lax.rsqrt(ms + NORM_EPS) * g_ref[...]).astype(BF16)
    proj = jnp.dot(xn, w_ref[...], preferred_element_type=F32)
    bd = bd_ref[...]
    rep = WIDTH_A // LANES
    cos = jnp.concatenate([cos_ref[...]] * rep, axis=1)
    sa = jnp.concatenate([sa_ref[...]] * rep, axis=1)
    sb = jnp.concatenate([sb_ref[...]] * rep, axis=1)
    half = ROT_DIM // 2

    def norm_rope(z, gain, scale):
        ssum = _group_sum(z * z, bd)
        zn = z * lax.rsqrt(ssum * (1.0 / HEAD_DIM) + NORM_EPS) * gain
        zr = (zn * cos + pltpu.roll(zn, WIDTH_A - half, 1) * sa + pltpu.roll(zn, half, 1) * sb)
        return zr * scale

    w = WIDTH_A
    qa_ref[...] = norm_rope(proj[:, 0:w], gains_ref[0:1, :], ATTN_SCALE).astype(BF16)
    ka_ref[...] = norm_rope(proj[:, w:2 * w], gains_ref[1:2, :], 1.0).astype(BF16)
    va_ref[...] = proj[:, 2 * w:3 * w].astype(BF16)
    qb_ref[...] = norm_rope(proj[:, 3 * w:4 * w], gains_ref[2:3, :], ATTN_SCALE).astype(BF16)
    kb_ref[...] = norm_rope(proj[:, 4 * w:5 * w], gains_ref[3:4, :], 1.0).astype(BF16)
    vb_ref[...] = proj[:, 5 * w:6 * w].astype(BF16)


def _inproj(x2, g, w_bf, gains, bd, cos_t, sa_t, sb_t, seq, tm=512):
    t, d = x2.shape
    ncol = w_bf.shape[1]
    nseq = seq // tm
    row = lambda i: (i, 0)
    fixed = lambda i: (0, 0)
    pos = lambda i: (i % nseq, 0)
    out_sds = jax.ShapeDtypeStruct((t, WIDTH_A), BF16)
    return pl.pallas_call(
        _inproj_kernel,
        grid=(t // tm,),
        in_specs=[pl.BlockSpec((tm, d), row), pl.BlockSpec((1, d), fixed),
                  pl.BlockSpec((d, ncol), fixed), pl.BlockSpec((4, WIDTH_A), fixed),
                  pl.BlockSpec((WIDTH_A, WIDTH_A), fixed),
                  pl.BlockSpec((tm, LANES), pos), pl.BlockSpec((tm, LANES), pos),
                  pl.BlockSpec((tm, LANES), pos)],
        out_specs=[pl.BlockSpec((tm, WIDTH_A), row)] * 6,
        out_shape=[out_sds] * 6,
        compiler_params=_cparams(("parallel",)),
        name="inproj",
    )(x2, g, w_bf, gains, bd, cos_t, sa_t, sb_t)


def _dilated_kernel(q_ref, kp_ref, k_ref, vp_ref, v_ref, o_ref, l_ref, kbuf, vbuf, *, tq):
    tile = pl.program_id(2)
    kbuf[0:BLOCK, :] = kp_ref[...]
    kbuf[BLOCK:, :] = k_ref[...]
    vbuf[0:BLOCK, :] = vp_ref[...]
    vbuf[BLOCK:, :] = v_ref[...]

    lane = lax.broadcasted_iota(jnp.int32, (BLOCK, LANES), 1)
    even = lane < HEAD_DIM
    head_masks = (even, lane >= HEAD_DIM)
    qi = lax.broadcasted_iota(jnp.int32, (BLOCK, 2 * BLOCK), 0)
    kj = lax.broadcasted_iota(jnp.int32, (BLOCK, 2 * BLOCK), 1)
    dist = qi + BLOCK - kj
    band = (dist >= 0) & (dist <= BLOCK)

    def body(blk, carry):
        r0 = pl.multiple_of(blk * BLOCK, BLOCK)
        first_key = jnp.where((tile == 0) & (blk == 0), BLOCK, 0)
        valid = band & (kj >= first_key)
        for pair in range(WIDTH_A // LANES):
            cols = slice(pair * LANES, (pair + 1) * LANES)
            q = q_ref[pl.ds(r0, BLOCK), cols]
            kk = kbuf[pl.ds(r0, 2 * BLOCK), cols]
            vv = vbuf[pl.ds(r0, 2 * BLOCK), cols]
            outs, lses = [], []
            for head_mask in head_masks:
                qh = jnp.where(head_mask, q, jnp.zeros_like(q))
                s = lax.dot_general(qh, kk, (((1,), (1,)), ((), ())), preferred_element_type=F32)
                s = jnp.where(valid, s, NEG_INF)
                m = jnp.max(s, axis=-1, keepdims=True)
                p = jnp.exp(s - m)
                den = jnp.sum(p, axis=-1, keepdims=True)
                o = jnp.dot(p.astype(BF16), vv, preferred_element_type=F32) / den
                outs.append(o)
                lses.append(jnp.broadcast_to(m + jnp.log(den), (BLOCK, LANES)))
            o_ref[pl.ds(r0, BLOCK), cols] = jnp.where(even, outs[0], outs[1])
            l_ref[pl.ds(r0, BLOCK), cols] = jnp.where(even, lses[0], lses[1])
        return carry

    lax.fori_loop(0, tq // BLOCK, body, 0)


def _dilated(qa, ka, va, batch, seq, dilation):
    stream = seq // dilation
    tq = min(stream, 1024)
    width = dilation * WIDTH_A
    view = lambda a: a.reshape(batch, stream, width)
    nprev = tq // BLOCK
    cur = lambda b, r, i: (b, i, r)
    prev = lambda b, r, i: (b, jnp.maximum(i * nprev - 1, 0), r)
    blk_cur = pl.BlockSpec((None, tq, WIDTH_A), cur)
    blk_prev = pl.BlockSpec((None, BLOCK, WIDTH_A), prev)
    out_sds = jax.ShapeDtypeStruct((batch, stream, width), F32)
    o, l = pl.pallas_call(
        functools.partial(_dilated_kernel, tq=tq),
        grid=(batch, dilation, stream // tq),
        in_specs=[blk_cur, blk_prev, blk_cur, blk_prev, blk_cur],
        out_specs=[blk_cur, blk_cur],
        out_shape=[out_sds, out_sds],
        scratch_shapes=[pltpu.VMEM((tq + BLOCK, WIDTH_A), BF16), pltpu.VMEM((tq + BLOCK, WIDTH_A), BF16)],
        compiler_params=_cparams(("parallel", "parallel", "arbitrary")),
        name=f"dilated_d{dilation}",
    )(view(qa), view(ka), view(ka), view(va), view(va))
    return o.reshape(batch * seq, WIDTH_A), l.reshape(batch * seq, WIDTH_A)


def _diff_kernel(lq1_ref, lk1_ref, lq2_ref, lk2_ref, gain_ref, q_ref, k_ref, v_ref, o_ref,
                 m_sc, l_sc, acc_sc, *, tq):
    i = pl.program_id(2)
    q = q_ref[...]
    lane = lax.broadcasted_iota(jnp.int32, q.shape, 1)
    zero = jnp.zeros_like(q)
    qs = (jnp.where(lane < DIFF_DIM, q, zero), jnp.where(lane >= DIFF_DIM, q, zero))

    m_sc[...] = jnp.full(m_sc.shape, NEG_INF, F32)
    l_sc[...] = jnp.zeros(l_sc.shape, F32)
    acc_sc[...] = jnp.zeros(acc_sc.shape, F32)

    row = lax.broadcasted_iota(jnp.int32, (tq, tq), 0)
    col = lax.broadcasted_iota(jnp.int32, (tq, tq), 1)
    causal = row >= col

    def step(j, masked):
        r0 = pl.multiple_of(j * tq, tq)
        kb = k_ref[pl.ds(r0, tq), :]
        vb = v_ref[pl.ds(r0, tq), :]
        for c in range(2):
            s = lax.dot_general(qs[c], kb, (((1,), (1,)), ((), ())), preferred_element_type=F32)
            if masked:
                s = jnp.where(causal, s, NEG_INF)
            m_old = m_sc[c]
            m_new = jnp.maximum(m_old, jnp.max(s, axis=-1, keepdims=True))
            alpha = jnp.exp(m_old - m_new)
            p = jnp.exp(s - m_new)
            l_sc[c] = alpha * l_sc[c] + jnp.sum(p, axis=-1, keepdims=True)
            acc_sc[c] = alpha * acc_sc[c] + jnp.dot(p.astype(BF16), vb, preferred_element_type=F32)
            m_sc[c] = m_new

    def body(j, carry):
        step(j, False)
        return carry

    lax.fori_loop(0, i, body, 0)
    step(i, True)

    lam = (jnp.exp(jnp.sum(lq1_ref[...] * lk1_ref[...], keepdims=True))
           - jnp.exp(jnp.sum(lq2_ref[...] * lk2_ref[...], keepdims=True)) + LAM_INIT)
    o = acc_sc[0] / l_sc[0] - lam * (acc_sc[1] / l_sc[1])
    ms = jnp.mean(o * o, axis=-1, keepdims=True)
    o = o * lax.rsqrt(ms + NORM_EPS) * gain_ref[...] * (1.0 - LAM_INIT)
    o_ref[...] = o.astype(o_ref.dtype)


def _diffattn(qb, kb, vb, lams, gain_b, batch, seq, tq=512):
    t = batch * seq
    nq = seq // tq
    lam_spec = pl.BlockSpec((1, DIFF_DIM), lambda b, h, i: (0, 0))
    kv_spec = pl.BlockSpec((seq, LANES), lambda b, h, i: (b, h))
    q_spec = pl.BlockSpec((tq, LANES), lambda b, h, i: (b * nq + i, h))
    return pl.pallas_call(
        functools.partial(_diff_kernel, tq=tq),
        grid=(batch, N_HEADS_B, nq),
        in_specs=[lam_spec] * 4 + [pl.BlockSpec((1, LANES), lambda b, h, i: (0, h)), q_spec, kv_spec, kv_spec],
        out_specs=q_spec,
        out_shape=jax.ShapeDtypeStruct((t, WIDTH_B), BF16),
        scratch_shapes=[pltpu.VMEM((2, tq, 1), F32), pltpu.VMEM((2, tq, 1), F32),
                        pltpu.VMEM((2, tq, DIFF_V_DIM), F32)],
        compiler_params=_cparams(("parallel", "parallel", "arbitrary")),
        name="diffattn",
    )(*lams, gain_b, qb, kb, vb)


def _outproj_kernel(x_ref, o1_ref, o2_ref, o3_ref, l1_ref, l2_ref, l3_ref, ob_ref, wa_ref, wb_ref,
                    gna_ref, bd_ref, gffn_ref, wq_ref, sk_ref, h_ref, hn_ref, sc_ref):
    l1, l2, l3 = l1_ref[...], l2_ref[...], l3_ref[...]
    lmax = jnp.maximum(jnp.maximum(l1, l2), l3)
    e1, e2, e3 = jnp.exp(l1 - lmax), jnp.exp(l2 - lmax), jnp.exp(l3 - lmax)
    oa = (e1 * o1_ref[...] + e2 * o2_ref[...] + e3 * o3_ref[...]) / (e1 + e2 + e3)
    ssum = _group_sum(oa * oa, bd_ref[...])
    oa = oa * lax.rsqrt(ssum * (1.0 / HEAD_DIM) + NORM_EPS) * gna_ref[...]
    h = (x_ref[...]
         + jnp.dot(oa.astype(BF16), wa_ref[...], preferred_element_type=F32)
         + jnp.dot(ob_ref[...], wb_ref[...], preferred_element_type=F32))
    h_ref[...] = h
    ms = jnp.mean(h * h, axis=-1, keepdims=True)
    hn = h * lax.rsqrt(ms + NORM_EPS) * gffn_ref[...]
    hn_ref[...] = hn
    q = jnp.dot(hn.astype(BF16), wq_ref[...], preferred_element_type=F32).astype(BF16)
    for hc in range(2 * PEER_HEADS):
        qs = q[:, hc * KEY_DIM:(hc + 1) * KEY_DIM]
        sc_ref[hc] = lax.dot_general(sk_ref[hc], qs, (((1,), (1,)), ((), ())), preferred_element_type=F32)


def _outproj(x2, o_l, ob, wa, wb, gna, bd, gffn, wq, sk, tm=256):
    t, d = x2.shape
    row = lambda i: (i, 0)
    fixed = lambda i: (0, 0)
    half = pl.BlockSpec((tm, WIDTH_A), row)
    full = pl.BlockSpec((tm, d), row)
    nq = wq.shape[1]
    return pl.pallas_call(
        _outproj_kernel,
        grid=(t // tm,),
        in_specs=[full] + [half] * 7
                 + [pl.BlockSpec((WIDTH_A, d), fixed), pl.BlockSpec((WIDTH_B, d), fixed),
                    pl.BlockSpec((1, WIDTH_A), fixed), pl.BlockSpec((WIDTH_A, WIDTH_A), fixed),
                    pl.BlockSpec((1, d), fixed), pl.BlockSpec((d, nq), fixed),
                    pl.BlockSpec((2 * PEER_HEADS, N_KEYS, KEY_DIM), lambda i: (0, 0, 0))],
        out_specs=[full, full, pl.BlockSpec((2 * PEER_HEADS, N_KEYS, tm), lambda i: (0, 0, i))],
        out_shape=[jax.ShapeDtypeStruct((t, d), F32), jax.ShapeDtypeStruct((t, d), F32),
                   jax.ShapeDtypeStruct((2 * PEER_HEADS, N_KEYS, t), F32)],
        compiler_params=_cparams(("parallel",)),
        name="outproj",
    )(x2, *o_l, ob, wa, wb, gna, bd, gffn, wq, sk)


def _extract_top(vals, payload, count):
    rows = vals.shape[0]
    iota = lax.broadcasted_iota(jnp.int32, vals.shape, 0)
    tops, picks = [], []
    for _ in range(count):
        m = jnp.max(vals, axis=0, keepdims=True)
        win = jnp.min(jnp.where(vals == m, iota, rows), axis=0, keepdims=True)
        hit = iota == win
        tops.append(m)
        picks.append(win if payload is None else jnp.sum(jnp.where(hit, payload, 0), axis=0, keepdims=True))
        vals = jnp.where(hit, NEG_INF, vals)
    return tops, picks


def _topk_kernel(sc_ref, idx_ref, gate_ref, idx_sc, gate_sc):
    for h in range(PEER_HEADS):
        v1, i1 = _extract_top(sc_ref[2 * h], None, PEER_TOPK)
        v2, i2 = _extract_top(sc_ref[2 * h + 1], None, PEER_TOPK)
        v2m = jnp.concatenate(v2, axis=0)
        i2m = jnp.concatenate(i2, axis=0)
        cand = jnp.concatenate([v1[a] + v2m for a in range(PEER_TOPK)], axis=0)
        cand_id = jnp.concatenate([(i1[a] * N_KEYS + i2m) * ROW_WORDS for a in range(PEER_TOPK)], axis=0)
        top, eid = _extract_top(cand, cand_id, PEER_TOPK)
        ex = [jnp.exp(tv - top[0]) for tv in top]
        den = ex[0]
        for e in ex[1:]:
            den = den + e
        for k in range(PEER_TOPK):
            j = h * PEER_TOPK + k
            idx_sc[j:j + 1, :] = eid[k]
            gate_sc[j:j + 1, :] = ex[k] / den
    idx_ref[0] = idx_sc[...].T
    gate_ref[0] = gate_sc[...].T


def _topk(scores_t, tt):
    t = scores_t.shape[-1]
    nt = t // tt
    out_spec = pl.BlockSpec((1, tt, PEER_SLOTS), lambda i: (i, 0, 0))
    return pl.pallas_call(
        _topk_kernel,
        grid=(nt,),
        in_specs=[pl.BlockSpec((2 * PEER_HEADS, N_KEYS, tt), lambda i: (0, 0, i))],
        out_specs=[out_spec, out_spec],
        out_shape=[jax.ShapeDtypeStruct((nt, tt, PEER_SLOTS), jnp.int32),
                   jax.ShapeDtypeStruct((nt, tt, PEER_SLOTS), F32)],
        scratch_shapes=[pltpu.VMEM((PEER_SLOTS, tt), jnp.int32), pltpu.VMEM((PEER_SLOTS, tt), F32)],
        compiler_params=_cparams(("parallel",)),
        name="topk",
    )(scores_t)


def _gather_rows(idx_ref, tab_ref, stage, t):
    slots = idx_ref.at[0, t]
    for j in range(PEER_SLOTS):
        e = pl.multiple_of(slots[j], ROW_WORDS)
        stage[j * ROW_WORDS:(j + 1) * ROW_WORDS, :] = tab_ref[pl.ds(e, ROW_WORDS), :]
    return pltpu.bitcast(stage[...], BF16)


def _chunk_mask():
    width = PEER_SLOTS * ROW_CHUNKS
    row = lax.broadcasted_iota(jnp.int32, (ROW_CHUNKS, width), 0)
    col = lax.broadcasted_iota(jnp.int32, (ROW_CHUNKS, width), 1)
    return (col & (ROW_CHUNKS - 1)) == row


def _stack_split(a):
    hi, lo = _split_bf16(a)
    return jnp.concatenate([hi, lo], axis=0)


def _peer_dot_kernel(idx_ref, x_ref, tab_ref, gate_ref, fold_ref, act_ref, *scratch, tt):
    stages, part_sc = scratch[:-1], scratch[-1]
    mask = _chunk_mask()

    def token(t, stage):
        rows = _gather_rows(idx_ref, tab_ref, stage, t)
        lhs = _stack_split(x_ref[t])
        r = lax.dot_general(lhs, rows, (((1,), (1,)), ((), ())), preferred_element_type=F32)
        r = r[0:ROW_CHUNKS] + r[ROW_CHUNKS:]
        part_sc[t] = jnp.where(mask, r, 0.0)

    def group(i, carry):
        for u, stage in enumerate(stages):
            token(i * len(stages) + u, stage)
        return carry

    lax.fori_loop(0, tt // len(stages), group, 0)
    part = part_sc[...].reshape(tt * ROW_CHUNKS, PEER_SLOTS * ROW_CHUNKS)
    hi, lo = _split_bf16(part)
    fold = fold_ref[...]
    z = (jnp.dot(hi, fold, preferred_element_type=F32) + jnp.dot(lo, fold, preferred_element_type=F32))
    z = jnp.sum(z.reshape(tt, ROW_CHUNKS, PEER_SLOTS), axis=1)
    gelu = 0.5 * z * (1.0 + lax.erf(z * (1.0 / math.sqrt(2.0))))
    act_ref[0] = gelu * gate_ref[0]


def _peer_dot(idx_t, hn3, tab, gate_t, fold, tt):
    nt = idx_t.shape[0]
    tile = lambda i: (i, 0, 0)
    slots = pl.BlockSpec((1, tt, PEER_SLOTS), tile)
    width = PEER_SLOTS * ROW_CHUNKS
    return pl.pallas_call(
        functools.partial(_peer_dot_kernel, tt=tt),
        grid=(nt,),
        in_specs=[pl.BlockSpec((1, tt, PEER_SLOTS), tile, memory_space=pltpu.SMEM),
                  pl.BlockSpec((tt, ROW_CHUNKS, LANES), tile),
                  pl.BlockSpec(memory_space=pltpu.VMEM),
                  slots,
                  pl.BlockSpec((width, PEER_SLOTS), lambda i: (0, 0))],
        out_specs=slots,
        out_shape=jax.ShapeDtypeStruct((nt, tt, PEER_SLOTS), F32),
        scratch_shapes=[pltpu.VMEM((PEER_SLOTS * ROW_WORDS, LANES), jnp.int32)] * PEER_GROUP
                       + [pltpu.VMEM((tt, ROW_CHUNKS, width), F32)],
        compiler_params=_cparams(("arbitrary",)),
        name="peer_dot",
    )(idx_t, hn3, tab, gate_t, fold)


def _peer_out_kernel(idx_ref, act_ref, tab_ref, h_ref, spread_ref, y_ref, *scratch, tt):
    stages, act_sc = scratch[:-1], scratch[-1]
    mask = _chunk_mask()
    hi, lo = _split_bf16(act_ref[0])
    spread = spread_ref[...]
    act_sc[...] = (jnp.dot(hi, spread, preferred_element_type=F32)
                   + jnp.dot(lo, spread, preferred_element_type=F32))

    def token(t, stage):
        rows = _gather_rows(idx_ref, tab_ref, stage, t)
        a = jnp.broadcast_to(act_sc[pl.ds(t, 1), :], mask.shape)
        lhs = _stack_split(jnp.where(mask, a, 0.0))
        o = jnp.dot(lhs, rows, preferred_element_type=F32)
        y_ref[t] = h_ref[t] + (o[0:ROW_CHUNKS] + o[ROW_CHUNKS:])

    def group(i, carry):
        for u, stage in enumerate(stages):
            token(i * len(stages) + u, stage)
        return carry

    lax.fori_loop(0, tt // len(stages), group, 0)


def _peer_out(idx_t, act_t, tab, h3, spread, tt):
    nt = idx_t.shape[0]
    tile = lambda i: (i, 0, 0)
    tok = pl.BlockSpec((tt, ROW_CHUNKS, LANES), tile)
    width = PEER_SLOTS * ROW_CHUNKS
    return pl.pallas_call(
        functools.partial(_peer_out_kernel, tt=tt),
        grid=(nt,),
        in_specs=[pl.BlockSpec((1, tt, PEER_SLOTS), tile, memory_space=pltpu.SMEM),
                  pl.BlockSpec((1, tt, PEER_SLOTS), tile),
                  pl.BlockSpec(memory_space=pltpu.VMEM), tok,
                  pl.BlockSpec((PEER_SLOTS, width), lambda i: (0, 0))],
        out_specs=tok,
        out_shape=jax.ShapeDtypeStruct(h3.shape, F32),
        scratch_shapes=[pltpu.VMEM((PEER_SLOTS * ROW_WORDS, LANES), jnp.int32)] * PEER_GROUP
                       + [pltpu.VMEM((tt, width), F32)],
        compiler_params=_cparams(("arbitrary",)),
        name="peer_out",
    )(idx_t, act_t, tab, h3, spread)


def _pack_table(tab):
    n, d = tab.shape
    bits = lax.bitcast_convert_type(tab.astype(BF16), jnp.uint16).astype(jnp.uint32)
    bits = bits.reshape(n, ROW_WORDS, 2, LANES)
    words = bits[:, :, 0, :] | (bits[:, :, 1, :] << 16)
    return lax.bitcast_convert_type(words, jnp.int32).reshape(n * ROW_WORDS, LANES)


def _rope_tables(seq):
    pos = jnp.arange(seq, dtype=F32)
    inv_freq = ROPE_THETA ** (-jnp.arange(0, ROT_DIM, 2, dtype=F32) / ROT_DIM)
    ang = pos[:, None] * inv_freq[None, :]
    cos, sin = jnp.cos(ang), jnp.sin(ang)
    half = ROT_DIM // 2
    pad1 = jnp.ones((seq, HEAD_DIM - ROT_DIM), F32)
    pad0 = jnp.zeros((seq, HEAD_DIM - ROT_DIM), F32)
    zeros = jnp.zeros((seq, half), F32)
    cos_h = jnp.concatenate([cos, cos, pad1], axis=1)
    sa_h = jnp.concatenate([-sin, zeros, pad0], axis=1)
    sb_h = jnp.concatenate([zeros, sin, pad0], axis=1)
    two = lambda a: jnp.concatenate([a, a], axis=1)
    return two(cos_h), two(sa_h), two(sb_h)


def _layer(x2, batch, seq, attn_norm_g, w_in, q_norm_a, k_norm_a, out_norm_a, q_norm_b, k_norm_b,
           lambda_q1, lambda_k1, lambda_q2, lambda_k2, out_norm_b, w_out, ffn_norm_g, w_peer_q,
           peer_sub_keys, peer_u, peer_v, peer_tt=128):
    d = x2.shape[1]
    tile_heads = lambda g: jnp.tile(g.astype(F32), WIDTH_A // HEAD_DIM)[None, :]
    gains = jnp.concatenate([tile_heads(q_norm_a), tile_heads(k_norm_a),
                             tile_heads(q_norm_b), tile_heads(k_norm_b)], axis=0)
    grp = jnp.arange(WIDTH_A) // HEAD_DIM
    bd = (grp[:, None] == grp[None, :]).astype(BF16)
    cos_t, sa_t, sb_t = _rope_tables(seq)

    qa, ka, va, qb, kb, vb = _inproj(x2, attn_norm_g[None, :], w_in.astype(BF16), gains, bd,
                                     cos_t, sa_t, sb_t, seq)
    outs = [_dilated(qa, ka, va, batch, seq, dil) for _, dil in DILATED_PATTERNS]
    lams = [v[None, :].astype(F32) for v in (lambda_q1, lambda_k1, lambda_q2, lambda_k2)]
    ob = _diffattn(qb, kb, vb, lams, out_norm_b[None, :].astype(F32), batch, seq)

    w_out_bf = w_out.astype(BF16)
    sk = peer_sub_keys.reshape(2 * PEER_HEADS, N_KEYS, KEY_DIM).astype(BF16)
    h, hn, scores_t = _outproj(x2, [o for o, _ in outs] + [l for _, l in outs], ob,
                               w_out_bf[:WIDTH_A], w_out_bf[WIDTH_A:], out_norm_a[None, :].astype(F32), bd,
                               ffn_norm_g[None, :].astype(F32), w_peer_q.astype(BF16), sk)
    idx_t, gate_t = _topk(scores_t, peer_tt)
    t = x2.shape[0]
    slot_of = jnp.arange(PEER_SLOTS * ROW_CHUNKS) // ROW_CHUNKS
    spread = (jnp.arange(PEER_SLOTS)[:, None] == slot_of[None, :]).astype(BF16)
    act_t = _peer_dot(idx_t, hn.reshape(t, ROW_CHUNKS, LANES), _pack_table(peer_u), gate_t, spread.T, peer_tt)
    y3 = _peer_out(idx_t, act_t, _pack_table(peer_v), h.reshape(t, ROW_CHUNKS, LANES), spread, peer_tt)
    return y3.reshape(t, d)


def kernel(x, attn_norm_g, w_in, q_norm_a, k_norm_a, out_norm_a, q_norm_b, k_norm_b, lambda_q1, lambda_k1,
           lambda_q2, lambda_k2, out_norm_b, w_out, ffn_norm_g, w_peer_q, peer_sub_keys, peer_u, peer_v):
    batch, seq, d = x.shape
    y = _layer(x.reshape(batch * seq, d), batch, seq, attn_norm_g[0], w_in[0], q_norm_a[0], k_norm_a[0],
               out_norm_a[0], q_norm_b[0], k_norm_b[0], lambda_q1[0], lambda_k1[0], lambda_q2[0],
               lambda_k2[0], out_norm_b[0], w_out[0], ffn_norm_g[0], w_peer_q[0], peer_sub_keys[0],
               peer_u[0], peer_v[0])
    return y.reshape(batch, seq, d)
```

```python
import functools
import math

import jax
import jax.numpy as jnp
from jax import lax
from jax.experimental import pallas as pl
from jax.experimental.pallas import tpu as pltpu

F32 = jnp.float32
BF16 = jnp.bfloat16

HEAD_DIM = 64
N_HEADS_A = 8
WIDTH_A = N_HEADS_A * HEAD_DIM
DILATED_PATTERNS = ((128, 1), (512, 4), (2048, 16))
N_HEADS_B = 4
DIFF_DIM = HEAD_DIM
DIFF_V_DIM = 2 * DIFF_DIM
WIDTH_B = N_HEADS_B * DIFF_V_DIM
ROPE_THETA = 500000.0
ROT_DIM = HEAD_DIM // 4
BLOCK = 128
NORM_EPS = 1e-6
ATTN_SCALE = 1.0 / math.sqrt(HEAD_DIM)
LAM_INIT = 0.8 - 0.6 * math.exp(-0.3 * 0)

PEER_HEADS = 8
N_KEYS = 128
KEY_DIM = 128
PEER_TOPK = 16
PEER_SLOTS = PEER_HEADS * PEER_TOPK
ROW_CHUNKS = 8
PEER_GROUP = 4

LANES = 128
VMEM_LIMIT = 56 * 1024 * 1024

NEG_INF = float("-inf")


def _cparams(sem):
    return pltpu.CompilerParams(dimension_semantics=sem, vmem_limit_bytes=VMEM_LIMIT)


def _split_bf16(a):
    hi = a.astype(BF16)
    lo = (a - hi.astype(F32)).astype(BF16)
    return hi, lo


def _group_sum(sq, bd):
    hi, lo = _split_bf16(sq)
    return (jnp.dot(hi, bd, preferred_element_type=F32)
            + jnp.dot(lo, bd, preferred_element_type=F32))


def _inproj_body(x_ref, g_ref, w_ref, gains_ref, bd_ref, cos_ref, sa_ref, sb_ref,
                 qa_refs, ka_refs, va_refs, qbt_ref, kb_ref, vbt_ref, stream_sc):
    x = x_ref[...]
    ms = jnp.mean(x * x, axis=-1, keepdims=True)
    xn = (x * lax.rsqrt(ms + NORM_EPS) * g_ref[...]).astype(BF16)
    proj = jnp.dot(xn, w_ref[...], preferred_element_type=F32)
    bd = bd_ref[...]
    rep = WIDTH_A // LANES
    cos = jnp.concatenate([cos_ref[...]] * rep, axis=1)
    sa = jnp.concatenate([sa_ref[...]] * rep, axis=1)
    sb = jnp.concatenate([sb_ref[...]] * rep, axis=1)
    half = ROT_DIM // 2

    def norm_rope(z, gain, scale):
        ssum = _group_sum(z * z, bd)
        zn = z * lax.rsqrt(ssum * (1.0 / HEAD_DIM) + NORM_EPS) * gain
        zr = (zn * cos + pltpu.roll(zn, WIDTH_A - half, 1) * sa + pltpu.roll(zn, half, 1) * sb)
        return zr * scale

    w = WIDTH_A
    tm = x.shape[0]
    mixer_a = (norm_rope(proj[:, 0:w], gains_ref[0:1, :], ATTN_SCALE),
               norm_rope(proj[:, w:2 * w], gains_ref[1:2, :], 1.0),
               proj[:, 2 * w:3 * w])
    for z, refs, sc in zip(mixer_a, (qa_refs, ka_refs, va_refs), stream_sc):
        nslab = w // LANES
        for k in range(nslab):
            sc[k] = z[:, k * LANES:(k + 1) * LANES]
        for (_, dil), ref in zip(DILATED_PATTERNS, refs):
            for r in range(dil):
                ref[r] = jnp.concatenate([sc[k, pl.ds(r, tm // dil, stride=dil), :] for k in range(nslab)],
                                         axis=1).astype(BF16)
    qbt_ref[...] = norm_rope(proj[:, 3 * w:4 * w], gains_ref[2:3, :], ATTN_SCALE).T.astype(BF16)
    kb_ref[...] = norm_rope(proj[:, 4 * w:5 * w], gains_ref[3:4, :], 1.0).astype(BF16)
    vbt_ref[...] = proj[:, 5 * w:6 * w].T.astype(BF16)


def _inproj_kernel(x_ref, g_ref, w_ref, gains_ref, bd_ref, cos_ref, sa_ref, sb_ref, *rest):
    nb = len(DILATED_PATTERNS)
    qa_refs, ka_refs, va_refs = rest[0:nb], rest[nb:2 * nb], rest[2 * nb:3 * nb]
    qbt_ref, kb_ref, vbt_ref = rest[3 * nb:3 * nb + 3]
    _inproj_body(x_ref, g_ref, w_ref, gains_ref, bd_ref, cos_ref, sa_ref, sb_ref,
                 qa_refs, ka_refs, va_refs, qbt_ref, kb_ref, vbt_ref, rest[3 * nb + 3:])


def _inproj(x2, g, w_bf, gains, bd, cos_t, sa_t, sb_t, batch, seq, tm=512):
    t, d = x2.shape
    ncol = w_bf.shape[1]
    nseq = seq // tm
    row = lambda i: (i, 0)
    fixed = lambda i: (0, 0)
    pos = lambda i: (i % nseq, 0)
    stream_specs, stream_sds = [], []
    for _, dil in DILATED_PATTERNS:
        stream_specs.append(pl.BlockSpec((None, dil, tm // dil, WIDTH_A), lambda i: (i // nseq, 0, i % nseq, 0)))
        stream_sds.append(jax.ShapeDtypeStruct((batch, dil, seq // dil, WIDTH_A), BF16))
    head_spec = pl.BlockSpec((None, WIDTH_B, tm), lambda i: (i // nseq, 0, i % nseq))
    head_sds = jax.ShapeDtypeStruct((batch, WIDTH_B, seq), BF16)
    outs = pl.pallas_call(
        _inproj_kernel,
        grid=(t // tm,),
        in_specs=[pl.BlockSpec((tm, d), row), pl.BlockSpec((1, d), fixed),
                  pl.BlockSpec((d, ncol), fixed), pl.BlockSpec((4, WIDTH_A), fixed),
                  pl.BlockSpec((WIDTH_A, WIDTH_A), fixed),
                  pl.BlockSpec((tm, LANES), pos), pl.BlockSpec((tm, LANES), pos),
                  pl.BlockSpec((tm, LANES), pos)],
        out_specs=stream_specs * 3 + [head_spec, pl.BlockSpec((tm, WIDTH_B), row), head_spec],
        out_shape=stream_sds * 3 + [head_sds, jax.ShapeDtypeStruct((t, WIDTH_B), BF16), head_sds],
        scratch_shapes=[pltpu.VMEM((WIDTH_A // LANES, tm, LANES), F32)] * 3,
        compiler_params=_cparams(("parallel",)),
        name="inproj",
    )(x2, g, w_bf, gains, bd, cos_t, sa_t, sb_t)
    nb = len(DILATED_PATTERNS)
    return outs[0:nb], outs[nb:2 * nb], outs[2 * nb:3 * nb], outs[3 * nb], outs[3 * nb + 1], outs[3 * nb + 2]


def _dilated_kernel(q_ref, kp_ref, k_ref, vp_ref, v_ref, o_ref, l_ref, kbuf, vbuf, *, tq):
    tile = pl.program_id(2)
    kbuf[0:BLOCK, :] = kp_ref[...]
    kbuf[BLOCK:, :] = k_ref[...]
    vbuf[0:BLOCK, :] = vp_ref[...]
    vbuf[BLOCK:, :] = v_ref[...]

    lane = lax.broadcasted_iota(jnp.int32, (BLOCK, LANES), 1)
    even = lane < HEAD_DIM
    head_masks = (even, lane >= HEAD_DIM)
    qi = lax.broadcasted_iota(jnp.int32, (BLOCK, 2 * BLOCK), 0)
    kj = lax.broadcasted_iota(jnp.int32, (BLOCK, 2 * BLOCK), 1)
    dist = qi + BLOCK - kj
    band = (dist >= 0) & (dist <= BLOCK)

    def body(blk, carry):
        r0 = pl.multiple_of(blk * BLOCK, BLOCK)
        first_key = jnp.where((tile == 0) & (blk == 0), BLOCK, 0)
        valid = band & (kj >= first_key)
        for pair in range(WIDTH_A // LANES):
            cols = slice(pair * LANES, (pair + 1) * LANES)
            q = q_ref[pl.ds(r0, BLOCK), cols]
            kk = kbuf[pl.ds(r0, 2 * BLOCK), cols]
            vv = vbuf[pl.ds(r0, 2 * BLOCK), cols]
            outs, lses = [], []
            for head_mask in head_masks:
                qh = jnp.where(head_mask, q, jnp.zeros_like(q))
                s = lax.dot_general(qh, kk, (((1,), (1,)), ((), ())), preferred_element_type=F32)
                s = jnp.where(valid, s, NEG_INF)
                m = jnp.max(s, axis=-1, keepdims=True)
                p = jnp.exp(s - m)
                den = jnp.sum(p, axis=-1, keepdims=True)
                o = jnp.dot(p.astype(BF16), vv, preferred_element_type=F32) / den
                outs.append(o)
                lses.append(jnp.broadcast_to(m + jnp.log(den), (BLOCK, LANES)))
            o_ref[pl.ds(r0, BLOCK), cols] = jnp.where(even, outs[0], outs[1])
            l_ref[pl.ds(r0, BLOCK), cols] = jnp.where(even, lses[0], lses[1])
        return carry

    lax.fori_loop(0, tq // BLOCK, body, 0)


def _dilated(qa, ka, va):
    batch, dilation, stream, _ = qa.shape
    tq = min(stream, 1024)
    nprev = tq // BLOCK
    cur = lambda b, r, i: (b, r, i, 0)
    prev = lambda b, r, i: (b, r, jnp.maximum(i * nprev - 1, 0), 0)
    blk_cur = pl.BlockSpec((None, None, tq, WIDTH_A), cur)
    blk_prev = pl.BlockSpec((None, None, BLOCK, WIDTH_A), prev)
    out_sds = jax.ShapeDtypeStruct(qa.shape, F32)
    return pl.pallas_call(
        functools.partial(_dilated_kernel, tq=tq),
        grid=(batch, dilation, stream // tq),
        in_specs=[blk_cur, blk_prev, blk_cur, blk_prev, blk_cur],
        out_specs=[blk_cur, blk_cur],
        out_shape=[out_sds, out_sds],
        scratch_shapes=[pltpu.VMEM((tq + BLOCK, WIDTH_A), BF16), pltpu.VMEM((tq + BLOCK, WIDTH_A), BF16)],
        compiler_params=_cparams(("parallel", "parallel", "arbitrary")),
        name=f"dilated_d{dilation}",
    )(qa, ka, ka, va, va)


def _diff_kernel(lq1_ref, lk1_ref, lq2_ref, lk2_ref, gain_ref, qt_ref, k_ref, vt_ref, o_ref,
                 m_sc, l_sc, acc_sc, *, tq):
    i = pl.program_id(2)
    qt = qt_ref[...]
    dim = lax.broadcasted_iota(jnp.int32, qt.shape, 0)
    zero = jnp.zeros_like(qt)
    qs = (jnp.where(dim < DIFF_DIM, qt, zero), jnp.where(dim >= DIFF_DIM, qt, zero))

    m_sc[...] = jnp.full(m_sc.shape, NEG_INF, F32)
    l_sc[...] = jnp.zeros(l_sc.shape, F32)
    acc_sc[...] = jnp.zeros(acc_sc.shape, F32)

    key = lax.broadcasted_iota(jnp.int32, (tq, tq), 0)
    query = lax.broadcasted_iota(jnp.int32, (tq, tq), 1)
    causal = key <= query

    def step(j, masked):
        r0 = pl.multiple_of(j * tq, tq)
        kb = k_ref[pl.ds(r0, tq), :]
        vtb = vt_ref[:, pl.ds(r0, tq)]
        for c in range(2):
            s = jnp.dot(kb, qs[c], preferred_element_type=F32)
            if masked:
                s = jnp.where(causal, s, NEG_INF)
            m_old = m_sc[c]
            m_new = jnp.maximum(m_old, jnp.max(s, axis=0, keepdims=True))
            alpha = jnp.exp(m_old - m_new)
            p = jnp.exp(s - m_new)
            l_sc[c] = alpha * l_sc[c] + jnp.sum(p, axis=0, keepdims=True)
            acc_sc[c] = alpha * acc_sc[c] + jnp.dot(vtb, p.astype(BF16), preferred_element_type=F32)
            m_sc[c] = m_new

    def body(j, carry):
        step(j, False)
        return carry

    lax.fori_loop(0, i, body, 0)
    step(i, True)

    lam = (jnp.exp(jnp.sum(lq1_ref[...] * lk1_ref[...], keepdims=True))
           - jnp.exp(jnp.sum(lq2_ref[...] * lk2_ref[...], keepdims=True)) + LAM_INIT)
    ot = acc_sc[0] / l_sc[0] - lam * (acc_sc[1] / l_sc[1])
    o = ot.T
    ms = jnp.mean(o * o, axis=-1, keepdims=True)
    o = o * lax.rsqrt(ms + NORM_EPS) * gain_ref[...] * (1.0 - LAM_INIT)
    o_ref[...] = o.astype(o_ref.dtype)


def _diffattn(qbt, kb, vbt, lams, gain_b, tq=512):
    batch, _, seq = qbt.shape
    t = batch * seq
    nq = seq // tq
    lam_spec = pl.BlockSpec((1, DIFF_DIM), lambda b, h, i: (0, 0))
    k_spec = pl.BlockSpec((seq, LANES), lambda b, h, i: (b, h))
    vt_spec = pl.BlockSpec((None, LANES, seq), lambda b, h, i: (b, h, 0))
    qt_spec = pl.BlockSpec((None, LANES, tq), lambda b, h, i: (b, h, i))
    o_spec = pl.BlockSpec((tq, LANES), lambda b, h, i: (b * nq + i, h))
    return pl.pallas_call(
        functools.partial(_diff_kernel, tq=tq),
        grid=(batch, N_HEADS_B, nq),
        in_specs=[lam_spec] * 4 + [pl.BlockSpec((1, LANES), lambda b, h, i: (0, h)), qt_spec, k_spec, vt_spec],
        out_specs=o_spec,
        out_shape=jax.ShapeDtypeStruct((t, WIDTH_B), BF16),
        scratch_shapes=[pltpu.VMEM((2, 1, tq), F32), pltpu.VMEM((2, 1, tq), F32),
                        pltpu.VMEM((2, DIFF_V_DIM, tq), F32)],
        compiler_params=_cparams(("parallel", "parallel", "arbitrary")),
        name="diffattn",
    )(*lams, gain_b, qbt, kb, vbt)


def _outproj_kernel(x_ref, o1_ref, o2_ref, o3_ref, l1_ref, l2_ref, l3_ref, ob_ref, wa_ref, wb_ref,
                    gna_ref, bd_ref, gffn_ref, wq_ref, sk_ref, h_ref, hn_ref, sc_ref, *order_sc):
    tm = x_ref.shape[0]

    def token_order(ref, sc):
        dil = ref.shape[0]
        if dil == 1:
            return ref[0]
        nslab = WIDTH_A // LANES
        for r in range(dil):
            for k in range(nslab):
                sc[k, pl.ds(r, tm // dil, stride=dil), :] = ref[r, :, k * LANES:(k + 1) * LANES]
        return jnp.concatenate([sc[k] for k in range(nslab)], axis=1)

    o1, o2, o3, l1, l2, l3 = [token_order(ref, sc) for ref, sc in
                              zip((o1_ref, o2_ref, o3_ref, l1_ref, l2_ref, l3_ref), order_sc)]
    lmax = jnp.maximum(jnp.maximum(l1, l2), l3)
    e1, e2, e3 = jnp.exp(l1 - lmax), jnp.exp(l2 - lmax), jnp.exp(l3 - lmax)
    oa = (e1 * o1 + e2 * o2 + e3 * o3) / (e1 + e2 + e3)
    ssum = _group_sum(oa * oa, bd_ref[...])
    oa = oa * lax.rsqrt(ssum * (1.0 / HEAD_DIM) + NORM_EPS) * gna_ref[...]
    h = (x_ref[...]
         + jnp.dot(oa.astype(BF16), wa_ref[...], preferred_element_type=F32)
         + jnp.dot(ob_ref[...], wb_ref[...], preferred_element_type=F32))
    h_ref[...] = h
    ms = jnp.mean(h * h, axis=-1, keepdims=True)
    hn = h * lax.rsqrt(ms + NORM_EPS) * gffn_ref[...]
    hn_ref[...] = hn
    q = jnp.dot(hn.astype(BF16), wq_ref[...], preferred_element_type=F32).astype(BF16)
    for hc in range(2 * PEER_HEADS):
        qs = q[:, hc * KEY_DIM:(hc + 1) * KEY_DIM]
        sc_ref[hc] = lax.dot_general(sk_ref[hc], qs, (((1,), (1,)), ((), ())), preferred_element_type=F32)


def _outproj(x2, o_l, ob, wa, wb, gna, bd, gffn, wq, sk, tm=256):
    t, d = x2.shape
    nseq = (o_l[0].shape[1] * o_l[0].shape[2]) // tm
    row = lambda i: (i, 0)
    fixed = lambda i: (0, 0)
    half = pl.BlockSpec((tm, WIDTH_A), row)
    full = pl.BlockSpec((tm, d), row)
    streams = [pl.BlockSpec((None, a.shape[1], tm // a.shape[1], WIDTH_A),
                            lambda i: (i // nseq, 0, i % nseq, 0)) for a in o_l]
    nq = wq.shape[1]
    return pl.pallas_call(
        _outproj_kernel,
        grid=(t // tm,),
        in_specs=[full] + streams + [half]
                 + [pl.BlockSpec((WIDTH_A, d), fixed), pl.BlockSpec((WIDTH_B, d), fixed),
                    pl.BlockSpec((1, WIDTH_A), fixed), pl.BlockSpec((WIDTH_A, WIDTH_A), fixed),
                    pl.BlockSpec((1, d), fixed), pl.BlockSpec((d, nq), fixed),
                    pl.BlockSpec((2 * PEER_HEADS, N_KEYS, KEY_DIM), lambda i: (0, 0, 0))],
        out_specs=[full, full, pl.BlockSpec((2 * PEER_HEADS, N_KEYS, tm), lambda i: (0, 0, i))],
        out_shape=[jax.ShapeDtypeStruct((t, d), F32), jax.ShapeDtypeStruct((t, d), F32),
                   jax.ShapeDtypeStruct((2 * PEER_HEADS, N_KEYS, t), F32)],
        scratch_shapes=[pltpu.VMEM((WIDTH_A // LANES, tm, LANES), F32)] * len(o_l),
        compiler_params=_cparams(("parallel",)),
        name="outproj",
    )(x2, *o_l, ob, wa, wb, gna, bd, gffn, wq, sk)


def _extract_top(vals, payload, count):
    rows = vals.shape[0]
    iota = lax.broadcasted_iota(jnp.int32, vals.shape, 0).astype(F32)
    tops, picks = [], []
    for _ in range(count):
        m = jnp.max(vals, axis=0, keepdims=True)
        win = jnp.min(jnp.where(vals == m, iota, float(rows)), axis=0, keepdims=True)
        hit = iota == win
        tops.append(m)
        picks.append(win.astype(jnp.int32) if payload is None
                     else jnp.sum(jnp.where(hit, payload, 0), axis=0, keepdims=True))
        vals = jnp.where(hit, NEG_INF, vals)
    return tops, picks


PAIR_COUNTS = tuple(PEER_TOPK // (a + 1) for a in range(PEER_TOPK))
N_PAIRS = sum(PAIR_COUNTS)
PAIR_ROWS = -(-N_PAIRS // 8) * 8


def _topk_kernel(sc_ref, idx_ref, gate_ref, idx_sc, gate_sc, cand_sc, cid_sc):
    tokens = cand_sc.shape[1]
    cand_sc[N_PAIRS:, :] = jnp.full((PAIR_ROWS - N_PAIRS, tokens), NEG_INF, F32)
    cid_sc[N_PAIRS:, :] = jnp.zeros((PAIR_ROWS - N_PAIRS, tokens), jnp.int32)
    for h in range(PEER_HEADS):
        v1, i1 = _extract_top(sc_ref[2 * h], None, PEER_TOPK)
        v2, i2 = _extract_top(sc_ref[2 * h + 1], None, PEER_TOPK)
        v2m = jnp.concatenate(v2, axis=0)
        i2m = jnp.concatenate(i2, axis=0)
        off = 0
        for a, nb in enumerate(PAIR_COUNTS):
            cand_sc[off:off + nb, :] = v1[a] + v2m[0:nb]
            cid_sc[off:off + nb, :] = i1[a] * N_KEYS + i2m[0:nb]
            off += nb
        top, eid = _extract_top(cand_sc[...], cid_sc[...], PEER_TOPK)
        ex = [jnp.exp(tv - top[0]) for tv in top]
        den = ex[0]
        for e in ex[1:]:
            den = den + e
        for k in range(PEER_TOPK):
            j = h * PEER_TOPK + k
            idx_sc[j:j + 1, :] = eid[k]
            gate_sc[j:j + 1, :] = ex[k] / den
    idx_ref[0] = idx_sc[...].T
    gate_ref[0] = gate_sc[...].T


def _topk(scores_t, tt):
    t = scores_t.shape[-1]
    nt = t // tt
    out_spec = pl.BlockSpec((1, tt, PEER_SLOTS), lambda i: (i, 0, 0))
    return pl.pallas_call(
        _topk_kernel,
        grid=(nt,),
        in_specs=[pl.BlockSpec((2 * PEER_HEADS, N_KEYS, tt), lambda i: (0, 0, i))],
        out_specs=[out_spec, out_spec],
        out_shape=[jax.ShapeDtypeStruct((nt, tt, PEER_SLOTS), jnp.int32),
                   jax.ShapeDtypeStruct((nt, tt, PEER_SLOTS), F32)],
        scratch_shapes=[pltpu.VMEM((PEER_SLOTS, tt), jnp.int32), pltpu.VMEM((PEER_SLOTS, tt), F32),
                        pltpu.VMEM((PAIR_ROWS, tt), F32), pltpu.VMEM((PAIR_ROWS, tt), jnp.int32)],
        compiler_params=_cparams(("parallel",)),
        name="topk",
    )(scores_t)


def _gather_rows(idx_ref, tab_ref, stage, t):
    slots = idx_ref.at[0, t]
    for j in range(PEER_SLOTS):
        stage[j * ROW_CHUNKS:(j + 1) * ROW_CHUNKS, :] = tab_ref[slots[j]]
    return stage[...]


def _chunk_mask():
    width = PEER_SLOTS * ROW_CHUNKS
    row = lax.broadcasted_iota(jnp.int32, (ROW_CHUNKS, width), 0)
    col = lax.broadcasted_iota(jnp.int32, (ROW_CHUNKS, width), 1)
    return (col & (ROW_CHUNKS - 1)) == row


def _stack_split(a):
    hi, lo = _split_bf16(a)
    return jnp.concatenate([hi, lo], axis=0)


def _to_chunk_major(x_ref, chunk_sc, tt):
    for c in range(ROW_CHUNKS):
        chunk_sc[pl.ds(c, tt, stride=ROW_CHUNKS), :] = x_ref[:, c * LANES:(c + 1) * LANES]


def _peer_dot_kernel(idx_ref, x_ref, tab_ref, gate_ref, fold_ref, act_ref, *scratch, tt):
    stages, x_sc, part_sc = scratch[:-2], scratch[-2], scratch[-1]
    mask = _chunk_mask()
    _to_chunk_major(x_ref, x_sc, tt)

    def token(t, stage):
        rows = _gather_rows(idx_ref, tab_ref, stage, t)
        xt = x_sc[pl.ds(pl.multiple_of(t * ROW_CHUNKS, ROW_CHUNKS), ROW_CHUNKS), :]
        lhs = _stack_split(xt)
        r = lax.dot_general(lhs, rows, (((1,), (1,)), ((), ())), preferred_element_type=F32)
        r = r[0:ROW_CHUNKS] + r[ROW_CHUNKS:]
        part_sc[t] = jnp.where(mask, r, 0.0)

    def group(i, carry):
        for u, stage in enumerate(stages):
            token(i * len(stages) + u, stage)
        return carry

    lax.fori_loop(0, tt // len(stages), group, 0)
    part = part_sc[...].reshape(tt * ROW_CHUNKS, PEER_SLOTS * ROW_CHUNKS)
    hi, lo = _split_bf16(part)
    fold = fold_ref[...]
    z = (jnp.dot(hi, fold, preferred_element_type=F32) + jnp.dot(lo, fold, preferred_element_type=F32))
    z = jnp.sum(z.reshape(tt, ROW_CHUNKS, PEER_SLOTS), axis=1)
    gelu = 0.5 * z * (1.0 + lax.erf(z * (1.0 / math.sqrt(2.0))))
    act_ref[0] = gelu * gate_ref[0]


def _peer_dot(idx_t, hn, tab, gate_t, fold, tt):
    nt = idx_t.shape[0]
    d = hn.shape[1]
    tile = lambda i: (i, 0, 0)
    slots = pl.BlockSpec((1, tt, PEER_SLOTS), tile)
    width = PEER_SLOTS * ROW_CHUNKS
    return pl.pallas_call(
        functools.partial(_peer_dot_kernel, tt=tt),
        grid=(nt,),
        in_specs=[pl.BlockSpec((1, tt, PEER_SLOTS), tile, memory_space=pltpu.SMEM),
                  pl.BlockSpec((tt, d), lambda i: (i, 0)),
                  pl.BlockSpec(memory_space=pltpu.VMEM),
                  slots,
                  pl.BlockSpec((width, PEER_SLOTS), lambda i: (0, 0))],
        out_specs=slots,
        out_shape=jax.ShapeDtypeStruct((nt, tt, PEER_SLOTS), F32),
        scratch_shapes=[pltpu.VMEM((PEER_SLOTS * ROW_CHUNKS, LANES), BF16)] * PEER_GROUP
                       + [pltpu.VMEM((tt * ROW_CHUNKS, LANES), F32), pltpu.VMEM((tt, ROW_CHUNKS, width), F32)],
        compiler_params=_cparams(("arbitrary",)),
        name="peer_dot",
    )(idx_t, hn, tab, gate_t, fold)


def _peer_out_kernel(idx_ref, act_ref, tab_ref, h_ref, spread_ref, y_ref, *scratch, tt):
    stages, y_sc, act_sc = scratch[:-2], scratch[-2], scratch[-1]
    mask = _chunk_mask()
    hi, lo = _split_bf16(act_ref[0])
    spread = spread_ref[...]
    act_sc[...] = (jnp.dot(hi, spread, preferred_element_type=F32)
                   + jnp.dot(lo, spread, preferred_element_type=F32))

    def token(t, stage):
        rows = _gather_rows(idx_ref, tab_ref, stage, t)
        a = jnp.broadcast_to(act_sc[pl.ds(t, 1), :], mask.shape)
        lhs = _stack_split(jnp.where(mask, a, 0.0))
        o = jnp.dot(lhs, rows, preferred_element_type=F32)
        y_sc[pl.ds(pl.multiple_of(t * ROW_CHUNKS, ROW_CHUNKS), ROW_CHUNKS), :] = o[0:ROW_CHUNKS] + o[ROW_CHUNKS:]

    def group(i, carry):
        for u, stage in enumerate(stages):
            token(i * len(stages) + u, stage)
        return carry

    lax.fori_loop(0, tt // len(stages), group, 0)
    for c in range(ROW_CHUNKS):
        cols = slice(c * LANES, (c + 1) * LANES)
        y_ref[:, cols] = h_ref[:, cols] + y_sc[pl.ds(c, tt, stride=ROW_CHUNKS), :]


def _peer_out(idx_t, act_t, tab, h, spread, tt):
    nt = idx_t.shape[0]
    tile = lambda i: (i, 0, 0)
    tok = pl.BlockSpec((tt, h.shape[1]), lambda i: (i, 0))
    width = PEER_SLOTS * ROW_CHUNKS
    return pl.pallas_call(
        functools.partial(_peer_out_kernel, tt=tt),
        grid=(nt,),
        in_specs=[pl.BlockSpec((1, tt, PEER_SLOTS), tile, memory_space=pltpu.SMEM),
                  pl.BlockSpec((1, tt, PEER_SLOTS), tile),
                  pl.BlockSpec(memory_space=pltpu.VMEM), tok,
                  pl.BlockSpec((PEER_SLOTS, width), lambda i: (0, 0))],
        out_specs=tok,
        out_shape=jax.ShapeDtypeStruct(h.shape, F32),
        scratch_shapes=[pltpu.VMEM((PEER_SLOTS * ROW_CHUNKS, LANES), BF16)] * PEER_GROUP
                       + [pltpu.VMEM((tt * ROW_CHUNKS, LANES), F32), pltpu.VMEM((tt, width), F32)],
        compiler_params=_cparams(("arbitrary",)),
        name="peer_out",
    )(idx_t, act_t, tab, h, spread)


def _chunk_table(tab):
    n, d = tab.shape
    return tab.astype(BF16).reshape(n, d // LANES, LANES)


def _rope_tables(seq):
    pos = jnp.arange(seq, dtype=F32)
    inv_freq = ROPE_THETA ** (-jnp.arange(0, ROT_DIM, 2, dtype=F32) / ROT_DIM)
    ang = pos[:, None] * inv_freq[None, :]
    cos, sin = jnp.cos(ang), jnp.sin(ang)
    half = ROT_DIM // 2
    pad1 = jnp.ones((seq, HEAD_DIM - ROT_DIM), F32)
    pad0 = jnp.zeros((seq, HEAD_DIM - ROT_DIM), F32)
    zeros = jnp.zeros((seq, half), F32)
    cos_h = jnp.concatenate([cos, cos, pad1], axis=1)
    sa_h = jnp.concatenate([-sin, zeros, pad0], axis=1)
    sb_h = jnp.concatenate([zeros, sin, pad0], axis=1)
    two = lambda a: jnp.concatenate([a, a], axis=1)
    return two(cos_h), two(sa_h), two(sb_h)


def _layer(x2, batch, seq, attn_norm_g, w_in, q_norm_a, k_norm_a, out_norm_a, q_norm_b, k_norm_b,
           lambda_q1, lambda_k1, lambda_q2, lambda_k2, out_norm_b, w_out, ffn_norm_g, w_peer_q,
           peer_sub_keys, peer_u, peer_v, peer_tt=128):
    tile_heads = lambda g: jnp.tile(g.astype(F32), WIDTH_A // HEAD_DIM)[None, :]
    gains = jnp.concatenate([tile_heads(q_norm_a), tile_heads(k_norm_a),
                             tile_heads(q_norm_b), tile_heads(k_norm_b)], axis=0)
    grp = jnp.arange(WIDTH_A) // HEAD_DIM
    bd = (grp[:, None] == grp[None, :]).astype(BF16)
    cos_t, sa_t, sb_t = _rope_tables(seq)

    qa, ka, va, qbt, kb, vbt = _inproj(x2, attn_norm_g[None, :].astype(F32), w_in.astype(BF16), gains, bd,
                                       cos_t, sa_t, sb_t, batch, seq)
    outs = [_dilated(q, k, v) for q, k, v in zip(qa, ka, va)]
    lams = [v[None, :].astype(F32) for v in (lambda_q1, lambda_k1, lambda_q2, lambda_k2)]
    ob = _diffattn(qbt, kb, vbt, lams, out_norm_b[None, :].astype(F32))

    w_out_bf = w_out.astype(BF16)
    sk = peer_sub_keys.reshape(2 * PEER_HEADS, N_KEYS, KEY_DIM).astype(BF16)
    h, hn, scores_t = _outproj(x2, [o for o, _ in outs] + [l for _, l in outs], ob,
                               w_out_bf[:WIDTH_A], w_out_bf[WIDTH_A:], out_norm_a[None, :].astype(F32), bd,
                               ffn_norm_g[None, :].astype(F32), w_peer_q.astype(BF16), sk)
    idx_t, gate_t = _topk(scores_t, peer_tt)
    slot_of = jnp.arange(PEER_SLOTS * ROW_CHUNKS) // ROW_CHUNKS
    spread = (jnp.arange(PEER_SLOTS)[:, None] == slot_of[None, :]).astype(BF16)
    act_t = _peer_dot(idx_t, hn, _chunk_table(peer_u), gate_t, spread.T, peer_tt)
    return _peer_out(idx_t, act_t, _chunk_table(peer_v), h, spread, peer_tt)


def kernel(x, attn_norm_g, w_in, q_norm_a, k_norm_a, out_norm_a, q_norm_b, k_norm_b, lambda_q1, lambda_k1,
           lambda_q2, lambda_k2, out_norm_b, w_out, ffn_norm_g, w_peer_q, peer_sub_keys, peer_u, peer_v):
    batch, seq, d = x.shape
    y = _layer(x.reshape(batch * seq, d), batch, seq, attn_norm_g[0], w_in[0], q_norm_a[0], k_norm_a[0],
               out_norm_a[0], q_norm_b[0], k_norm_b[0], lambda_q1[0], lambda_k1[0], lambda_q2[0],
               lambda_k2[0], out_norm_b[0], w_out[0], ffn_norm_g[0], w_peer_q[0], peer_sub_keys[0],
               peer_u[0], peer_v[0])
    return y.reshape(batch, seq, d)
```

```python
import functools
import math

import jax
import jax.numpy as jnp
from jax import lax
from jax.experimental import pallas as pl
from jax.experimental.pallas import tpu as pltpu

F32 = jnp.float32
BF16 = jnp.bfloat16

HEAD_DIM = 64
N_HEADS_A = 8
WIDTH_A = N_HEADS_A * HEAD_DIM
DILATED_PATTERNS = ((128, 1), (512, 4), (2048, 16))
N_HEADS_B = 4
DIFF_DIM = HEAD_DIM
DIFF_V_DIM = 2 * DIFF_DIM
WIDTH_B = N_HEADS_B * DIFF_V_DIM
ROPE_THETA = 500000.0
ROT_DIM = HEAD_DIM // 4
BLOCK = 128
NORM_EPS = 1e-6
ATTN_SCALE = 1.0 / math.sqrt(HEAD_DIM)
LAM_INIT = 0.8 - 0.6 * math.exp(-0.3 * 0)

PEER_HEADS = 8
N_KEYS = 128
KEY_DIM = 128
PEER_TOPK = 16
PEER_SLOTS = PEER_HEADS * PEER_TOPK
ROW_CHUNKS = 8
PEER_GROUP = 8

LANES = 128
VMEM_LIMIT = 56 * 1024 * 1024

NEG_INF = float("-inf")


def _cparams(sem):
    return pltpu.CompilerParams(dimension_semantics=sem, vmem_limit_bytes=VMEM_LIMIT)


def _split_bf16(a):
    hi = a.astype(BF16)
    lo = (a - hi.astype(F32)).astype(BF16)
    return hi, lo


def _group_sum(sq, bd):
    hi, lo = _split_bf16(sq)
    return (jnp.dot(hi, bd, preferred_element_type=F32)
            + jnp.dot(lo, bd, preferred_element_type=F32))


def _inproj_body(x_ref, g_ref, w_ref, gains_ref, bd_ref, cos_ref, sa_ref, sb_ref,
                 qa_refs, ka_refs, va_refs, qbt_ref, kb_ref, vbt_ref, stream_sc):
    x = x_ref[...]
    ms = jnp.mean(x * x, axis=-1, keepdims=True)
    xn = (x * lax.rsqrt(ms + NORM_EPS) * g_ref[...]).astype(BF16)
    proj = jnp.dot(xn, w_ref[...], preferred_element_type=F32)
    bd = bd_ref[...]
    rep = WIDTH_A // LANES
    cos = jnp.concatenate([cos_ref[...]] * rep, axis=1)
    sa = jnp.concatenate([sa_ref[...]] * rep, axis=1)
    sb = jnp.concatenate([sb_ref[...]] * rep, axis=1)
    half = ROT_DIM // 2

    def norm_rope(z, gain, scale):
        ssum = _group_sum(z * z, bd)
        zn = z * lax.rsqrt(ssum * (1.0 / HEAD_DIM) + NORM_EPS) * gain
        zr = (zn * cos + pltpu.roll(zn, WIDTH_A - half, 1) * sa + pltpu.roll(zn, half, 1) * sb)
        return zr * scale

    w = WIDTH_A
    tm = x.shape[0]
    mixer_a = (norm_rope(proj[:, 0:w], gains_ref[0:1, :], ATTN_SCALE),
               norm_rope(proj[:, w:2 * w], gains_ref[1:2, :], 1.0),
               proj[:, 2 * w:3 * w])
    for z, refs, sc in zip(mixer_a, (qa_refs, ka_refs, va_refs), stream_sc):
        nslab = w // LANES
        for k in range(nslab):
            sc[k] = z[:, k * LANES:(k + 1) * LANES]
        for (_, dil), ref in zip(DILATED_PATTERNS, refs):
            for r in range(dil):
                ref[r] = jnp.concatenate([sc[k, pl.ds(r, tm // dil, stride=dil), :] for k in range(nslab)],
                                         axis=1).astype(BF16)
    qbt_ref[...] = norm_rope(proj[:, 3 * w:4 * w], gains_ref[2:3, :], ATTN_SCALE).T.astype(BF16)
    kb_ref[...] = norm_rope(proj[:, 4 * w:5 * w], gains_ref[3:4, :], 1.0).astype(BF16)
    vbt_ref[...] = proj[:, 5 * w:6 * w].T.astype(BF16)


def _inproj_kernel(x_ref, g_ref, w_ref, gains_ref, bd_ref, cos_ref, sa_ref, sb_ref, *rest):
    nb = len(DILATED_PATTERNS)
    qa_refs, ka_refs, va_refs = rest[0:nb], rest[nb:2 * nb], rest[2 * nb:3 * nb]
    qbt_ref, kb_ref, vbt_ref = rest[3 * nb:3 * nb + 3]
    _inproj_body(x_ref, g_ref, w_ref, gains_ref, bd_ref, cos_ref, sa_ref, sb_ref,
                 qa_refs, ka_refs, va_refs, qbt_ref, kb_ref, vbt_ref, rest[3 * nb + 3:])


def _inproj(x2, g, w_bf, gains, bd, cos_t, sa_t, sb_t, batch, seq, tm=512):
    t, d = x2.shape
    ncol = w_bf.shape[1]
    nseq = seq // tm
    row = lambda i: (i, 0)
    fixed = lambda i: (0, 0)
    pos = lambda i: (i % nseq, 0)
    stream_specs, stream_sds = [], []
    for _, dil in DILATED_PATTERNS:
        stream_specs.append(pl.BlockSpec((None, dil, tm // dil, WIDTH_A), lambda i: (i // nseq, 0, i % nseq, 0)))
        stream_sds.append(jax.ShapeDtypeStruct((batch, dil, seq // dil, WIDTH_A), BF16))
    head_spec = pl.BlockSpec((None, WIDTH_B, tm), lambda i: (i // nseq, 0, i % nseq))
    head_sds = jax.ShapeDtypeStruct((batch, WIDTH_B, seq), BF16)
    outs = pl.pallas_call(
        _inproj_kernel,
        grid=(t // tm,),
        in_specs=[pl.BlockSpec((tm, d), row), pl.BlockSpec((1, d), fixed),
                  pl.BlockSpec((d, ncol), fixed), pl.BlockSpec((4, WIDTH_A), fixed),
                  pl.BlockSpec((WIDTH_A, WIDTH_A), fixed),
                  pl.BlockSpec((tm, LANES), pos), pl.BlockSpec((tm, LANES), pos),
                  pl.BlockSpec((tm, LANES), pos)],
        out_specs=stream_specs * 3 + [head_spec, pl.BlockSpec((tm, WIDTH_B), row), head_spec],
        out_shape=stream_sds * 3 + [head_sds, jax.ShapeDtypeStruct((t, WIDTH_B), BF16), head_sds],
        scratch_shapes=[pltpu.VMEM((WIDTH_A // LANES, tm, LANES), F32)] * 3,
        compiler_params=_cparams(("parallel",)),
        name="inproj",
    )(x2, g, w_bf, gains, bd, cos_t, sa_t, sb_t)
    nb = len(DILATED_PATTERNS)
    return outs[0:nb], outs[nb:2 * nb], outs[2 * nb:3 * nb], outs[3 * nb], outs[3 * nb + 1], outs[3 * nb + 2]


def _dilated_kernel(q_ref, kp_ref, k_ref, vp_ref, v_ref, o_ref, l_ref, kbuf, vbuf, *, tq):
    tile = pl.program_id(2)
    kbuf[0:BLOCK, :] = kp_ref[...]
    kbuf[BLOCK:, :] = k_ref[...]
    vbuf[0:BLOCK, :] = vp_ref[...]
    vbuf[BLOCK:, :] = v_ref[...]

    lane = lax.broadcasted_iota(jnp.int32, (BLOCK, LANES), 1)
    even = lane < HEAD_DIM
    head_masks = (even, lane >= HEAD_DIM)
    qi = lax.broadcasted_iota(jnp.int32, (BLOCK, 2 * BLOCK), 0)
    kj = lax.broadcasted_iota(jnp.int32, (BLOCK, 2 * BLOCK), 1)
    dist = qi + BLOCK - kj
    band = (dist >= 0) & (dist <= BLOCK)

    def body(blk, carry):
        r0 = pl.multiple_of(blk * BLOCK, BLOCK)
        first_key = jnp.where((tile == 0) & (blk == 0), BLOCK, 0)
        valid = band & (kj >= first_key)
        for pair in range(WIDTH_A // LANES):
            cols = slice(pair * LANES, (pair + 1) * LANES)
            q = q_ref[pl.ds(r0, BLOCK), cols]
            kk = kbuf[pl.ds(r0, 2 * BLOCK), cols]
            vv = vbuf[pl.ds(r0, 2 * BLOCK), cols]
            outs, lses = [], []
            for head_mask in head_masks:
                qh = jnp.where(head_mask, q, jnp.zeros_like(q))
                s = lax.dot_general(qh, kk, (((1,), (1,)), ((), ())), preferred_element_type=F32)
                s = jnp.where(valid, s, NEG_INF)
                m = jnp.max(s, axis=-1, keepdims=True)
                p = jnp.exp(s - m)
                den = jnp.sum(p, axis=-1, keepdims=True)
                o = jnp.dot(p.astype(BF16), vv, preferred_element_type=F32) / den
                outs.append(o)
                lses.append(jnp.broadcast_to(m + jnp.log(den), (BLOCK, LANES)))
            o_ref[pl.ds(r0, BLOCK), cols] = jnp.where(even, outs[0], outs[1])
            l_ref[pl.ds(r0, BLOCK), cols] = jnp.where(even, lses[0], lses[1])
        return carry

    lax.fori_loop(0, tq // BLOCK, body, 0)


def _dilated(qa, ka, va):
    batch, dilation, stream, _ = qa.shape
    tq = min(stream, 1024)
    nprev = tq // BLOCK
    cur = lambda b, r, i: (b, r, i, 0)
    prev = lambda b, r, i: (b, r, jnp.maximum(i * nprev - 1, 0), 0)
    blk_cur = pl.BlockSpec((None, None, tq, WIDTH_A), cur)
    blk_prev = pl.BlockSpec((None, None, BLOCK, WIDTH_A), prev)
    out_sds = jax.ShapeDtypeStruct(qa.shape, F32)
    return pl.pallas_call(
        functools.partial(_dilated_kernel, tq=tq),
        grid=(batch, dilation, stream // tq),
        in_specs=[blk_cur, blk_prev, blk_cur, blk_prev, blk_cur],
        out_specs=[blk_cur, blk_cur],
        out_shape=[out_sds, out_sds],
        scratch_shapes=[pltpu.VMEM((tq + BLOCK, WIDTH_A), BF16), pltpu.VMEM((tq + BLOCK, WIDTH_A), BF16)],
        compiler_params=_cparams(("parallel", "parallel", "arbitrary")),
        name=f"dilated_d{dilation}",
    )(qa, ka, ka, va, va)


def _diff_kernel(lq1_ref, lk1_ref, lq2_ref, lk2_ref, gain_ref, qt_ref, k_ref, vt_ref, o_ref,
                 m_sc, l_sc, acc_sc, *, tq):
    i = pl.program_id(2)
    heads = qt_ref.shape[0] // DIFF_V_DIM
    dim = lax.broadcasted_iota(jnp.int32, (DIFF_V_DIM, tq), 0)
    qs = []
    for hh in range(heads):
        qt = qt_ref[hh * DIFF_V_DIM:(hh + 1) * DIFF_V_DIM, :]
        zero = jnp.zeros_like(qt)
        qs += [jnp.where(dim < DIFF_DIM, qt, zero), jnp.where(dim >= DIFF_DIM, qt, zero)]

    m_sc[...] = jnp.full(m_sc.shape, NEG_INF, F32)
    l_sc[...] = jnp.zeros(l_sc.shape, F32)
    acc_sc[...] = jnp.zeros(acc_sc.shape, F32)

    key = lax.broadcasted_iota(jnp.int32, (tq, tq), 0)
    query = lax.broadcasted_iota(jnp.int32, (tq, tq), 1)
    causal = key <= query

    def step(j, masked):
        r0 = pl.multiple_of(j * tq, tq)
        for hh in range(heads):
            cols = slice(hh * DIFF_V_DIM, (hh + 1) * DIFF_V_DIM)
            kb = k_ref[pl.ds(r0, tq), cols]
            vtb = vt_ref[cols, pl.ds(r0, tq)]
            for c in range(2 * hh, 2 * hh + 2):
                s = jnp.dot(kb, qs[c], preferred_element_type=F32)
                if masked:
                    s = jnp.where(causal, s, NEG_INF)
                m_old = m_sc[c]
                m_new = jnp.maximum(m_old, jnp.max(s, axis=0, keepdims=True))
                alpha = jnp.exp(m_old - m_new)
                p = jnp.exp(s - m_new)
                l_sc[c] = alpha * l_sc[c] + jnp.sum(p, axis=0, keepdims=True)
                acc_sc[c] = alpha * acc_sc[c] + jnp.dot(vtb, p.astype(BF16), preferred_element_type=F32)
                m_sc[c] = m_new

    def body(j, carry):
        step(j, False)
        return carry

    lax.fori_loop(0, i, body, 0)
    step(i, True)

    lam = (jnp.exp(jnp.sum(lq1_ref[...] * lk1_ref[...], keepdims=True))
           - jnp.exp(jnp.sum(lq2_ref[...] * lk2_ref[...], keepdims=True)) + LAM_INIT)
    for hh in range(heads):
        cols = slice(hh * DIFF_V_DIM, (hh + 1) * DIFF_V_DIM)
        ot = acc_sc[2 * hh] / l_sc[2 * hh] - lam * (acc_sc[2 * hh + 1] / l_sc[2 * hh + 1])
        o = ot.T
        ms = jnp.mean(o * o, axis=-1, keepdims=True)
        o = o * lax.rsqrt(ms + NORM_EPS) * gain_ref[:, cols] * (1.0 - LAM_INIT)
        o_ref[:, cols] = o.astype(o_ref.dtype)


def _diffattn(qbt, kb, vbt, lams, gain_b, tq=512, heads=2):
    batch, _, seq = qbt.shape
    t = batch * seq
    nq = seq // tq
    width = heads * DIFF_V_DIM
    lam_spec = pl.BlockSpec((1, DIFF_DIM), lambda b, h, i: (0, 0))
    k_spec = pl.BlockSpec((seq, width), lambda b, h, i: (b, h))
    vt_spec = pl.BlockSpec((None, width, seq), lambda b, h, i: (b, h, 0))
    qt_spec = pl.BlockSpec((None, width, tq), lambda b, h, i: (b, h, i))
    o_spec = pl.BlockSpec((tq, width), lambda b, h, i: (b * nq + i, h))
    return pl.pallas_call(
        functools.partial(_diff_kernel, tq=tq),
        grid=(batch, N_HEADS_B // heads, nq),
        in_specs=[lam_spec] * 4 + [pl.BlockSpec((1, width), lambda b, h, i: (0, h)), qt_spec, k_spec, vt_spec],
        out_specs=o_spec,
        out_shape=jax.ShapeDtypeStruct((t, WIDTH_B), BF16),
        scratch_shapes=[pltpu.VMEM((2 * heads, 1, tq), F32), pltpu.VMEM((2 * heads, 1, tq), F32),
                        pltpu.VMEM((2 * heads, DIFF_V_DIM, tq), F32)],
        compiler_params=_cparams(("parallel", "parallel", "arbitrary")),
        name="diffattn",
    )(*lams, gain_b, qbt, kb, vbt)


def _outproj_kernel(x_ref, o1_ref, o2_ref, o3_ref, l1_ref, l2_ref, l3_ref, ob_ref, wa_ref, wb_ref,
                    gna_ref, bd_ref, gffn_ref, wq_ref, sk_ref, h_ref, hn_ref, sc_ref, *order_sc):
    tm = x_ref.shape[0]

    def token_order(ref, sc):
        dil = ref.shape[0]
        if dil == 1:
            return ref[0]
        nslab = WIDTH_A // LANES
        for r in range(dil):
            for k in range(nslab):
                sc[k, pl.ds(r, tm // dil, stride=dil), :] = ref[r, :, k * LANES:(k + 1) * LANES]
        return jnp.concatenate([sc[k] for k in range(nslab)], axis=1)

    o1, o2, o3, l1, l2, l3 = [token_order(ref, sc) for ref, sc in
                              zip((o1_ref, o2_ref, o3_ref, l1_ref, l2_ref, l3_ref), order_sc)]
    lmax = jnp.maximum(jnp.maximum(l1, l2), l3)
    e1, e2, e3 = jnp.exp(l1 - lmax), jnp.exp(l2 - lmax), jnp.exp(l3 - lmax)
    oa = (e1 * o1 + e2 * o2 + e3 * o3) / (e1 + e2 + e3)
    ssum = _group_sum(oa * oa, bd_ref[...])
    oa = oa * lax.rsqrt(ssum * (1.0 / HEAD_DIM) + NORM_EPS) * gna_ref[...]
    h = (x_ref[...]
         + jnp.dot(oa.astype(BF16), wa_ref[...], preferred_element_type=F32)
         + jnp.dot(ob_ref[...], wb_ref[...], preferred_element_type=F32))
    h_ref[...] = h
    ms = jnp.mean(h * h, axis=-1, keepdims=True)
    hn = h * lax.rsqrt(ms + NORM_EPS) * gffn_ref[...]
    hn_ref[...] = hn
    q = jnp.dot(hn.astype(BF16), wq_ref[...], preferred_element_type=F32).astype(BF16)
    for hc in range(2 * PEER_HEADS):
        qs = q[:, hc * KEY_DIM:(hc + 1) * KEY_DIM]
        sc_ref[hc] = lax.dot_general(sk_ref[hc], qs, (((1,), (1,)), ((), ())), preferred_element_type=F32)


def _outproj(x2, o_l, ob, wa, wb, gna, bd, gffn, wq, sk, tm=256):
    t, d = x2.shape
    nseq = (o_l[0].shape[1] * o_l[0].shape[2]) // tm
    row = lambda i: (i, 0)
    fixed = lambda i: (0, 0)
    half = pl.BlockSpec((tm, WIDTH_A), row)
    full = pl.BlockSpec((tm, d), row)
    streams = [pl.BlockSpec((None, a.shape[1], tm // a.shape[1], WIDTH_A),
                            lambda i: (i // nseq, 0, i % nseq, 0)) for a in o_l]
    nq = wq.shape[1]
    return pl.pallas_call(
        _outproj_kernel,
        grid=(t // tm,),
        in_specs=[full] + streams + [half]
                 + [pl.BlockSpec((WIDTH_A, d), fixed), pl.BlockSpec((WIDTH_B, d), fixed),
                    pl.BlockSpec((1, WIDTH_A), fixed), pl.BlockSpec((WIDTH_A, WIDTH_A), fixed),
                    pl.BlockSpec((1, d), fixed), pl.BlockSpec((d, nq), fixed),
                    pl.BlockSpec((2 * PEER_HEADS, N_KEYS, KEY_DIM), lambda i: (0, 0, 0))],
        out_specs=[full, full, pl.BlockSpec((2 * PEER_HEADS, N_KEYS, tm), lambda i: (0, 0, i))],
        out_shape=[jax.ShapeDtypeStruct((t, d), F32), jax.ShapeDtypeStruct((t, d), F32),
                   jax.ShapeDtypeStruct((2 * PEER_HEADS, N_KEYS, t), F32)],
        scratch_shapes=[pltpu.VMEM((WIDTH_A // LANES, tm, LANES), F32)] * len(o_l),
        compiler_params=_cparams(("parallel",)),
        name="outproj",
    )(x2, *o_l, ob, wa, wb, gna, bd, gffn, wq, sk)


def _extract_top(vals, payload, count):
    rows = vals.shape[0]
    iota = lax.broadcasted_iota(jnp.int32, vals.shape, 0).astype(F32)
    tops, picks = [], []
    for _ in range(count):
        m = jnp.max(vals, axis=0, keepdims=True)
        win = jnp.min(jnp.where(vals == m, iota, float(rows)), axis=0, keepdims=True)
        hit = iota == win
        tops.append(m)
        picks.append(win.astype(jnp.int32) if payload is None
                     else jnp.sum(jnp.where(hit, payload, 0), axis=0, keepdims=True))
        vals = jnp.where(hit, NEG_INF, vals)
    return tops, picks


PAIR_COUNTS = tuple(PEER_TOPK // (a + 1) for a in range(PEER_TOPK))
N_PAIRS = sum(PAIR_COUNTS)
PAIR_ROWS = -(-N_PAIRS // 8) * 8


def _topk_kernel(sc_ref, idx_ref, gate_ref, idx_sc, gate_sc, cand_sc, cid_sc):
    tokens = cand_sc.shape[1]
    cand_sc[N_PAIRS:, :] = jnp.full((PAIR_ROWS - N_PAIRS, tokens), NEG_INF, F32)
    cid_sc[N_PAIRS:, :] = jnp.zeros((PAIR_ROWS - N_PAIRS, tokens), jnp.int32)
    for h in range(PEER_HEADS):
        v1, i1 = _extract_top(sc_ref[2 * h], None, PEER_TOPK)
        v2, i2 = _extract_top(sc_ref[2 * h + 1], None, PEER_TOPK)
        v2m = jnp.concatenate(v2, axis=0)
        i2m = jnp.concatenate(i2, axis=0)
        off = 0
        for a, nb in enumerate(PAIR_COUNTS):
            cand_sc[off:off + nb, :] = v1[a] + v2m[0:nb]
            cid_sc[off:off + nb, :] = i1[a] * N_KEYS + i2m[0:nb]
            off += nb
        top, eid = _extract_top(cand_sc[...], cid_sc[...], PEER_TOPK)
        ex = [jnp.exp(tv - top[0]) for tv in top]
        den = ex[0]
        for e in ex[1:]:
            den = den + e
        for k in range(PEER_TOPK):
            j = h * PEER_TOPK + k
            idx_sc[j:j + 1, :] = eid[k]
            gate_sc[j:j + 1, :] = ex[k] / den
    idx_ref[0] = idx_sc[...].T
    gate_ref[0] = gate_sc[...].T


def _topk(scores_t, tt):
    t = scores_t.shape[-1]
    nt = t // tt
    out_spec = pl.BlockSpec((1, tt, PEER_SLOTS), lambda i: (i, 0, 0))
    return pl.pallas_call(
        _topk_kernel,
        grid=(nt,),
        in_specs=[pl.BlockSpec((2 * PEER_HEADS, N_KEYS, tt), lambda i: (0, 0, i))],
        out_specs=[out_spec, out_spec],
        out_shape=[jax.ShapeDtypeStruct((nt, tt, PEER_SLOTS), jnp.int32),
                   jax.ShapeDtypeStruct((nt, tt, PEER_SLOTS), F32)],
        scratch_shapes=[pltpu.VMEM((PEER_SLOTS, tt), jnp.int32), pltpu.VMEM((PEER_SLOTS, tt), F32),
                        pltpu.VMEM((PAIR_ROWS, tt), F32), pltpu.VMEM((PAIR_ROWS, tt), jnp.int32)],
        compiler_params=_cparams(("parallel",)),
        name="topk",
    )(scores_t)


def _gather_rows(slot_ids, tab_ref, stage):
    for j in range(PEER_SLOTS):
        stage[j * ROW_CHUNKS:(j + 1) * ROW_CHUNKS, :] = tab_ref[slot_ids[j]]
    return stage[...]


def _for_each_token(idx_ref, idx_bufs, sems, stages, token_fn, tt):
    group = len(stages)
    npairs = tt // (2 * group)

    def ids_copy(g, k):
        return pltpu.make_async_copy(idx_ref.at[0, pl.ds(g * group, group)], idx_bufs[k], sems.at[k])

    def run_group(g, k):
        for u, stage in enumerate(stages):
            token_fn(g * group + u, idx_bufs[k].at[u], stage)

    ids_copy(0, 0).start()

    def pair(p, carry):
        g = 2 * p
        ids_copy(g, 0).wait()
        ids_copy(g + 1, 1).start()
        run_group(g, 0)
        ids_copy(g + 1, 1).wait()

        @pl.when(p + 1 < npairs)
        def _():
            ids_copy(g + 2, 0).start()

        run_group(g + 1, 1)
        return carry

    lax.fori_loop(0, npairs, pair, 0)


def _gather_scratch():
    return ([pltpu.VMEM((PEER_SLOTS * ROW_CHUNKS, LANES), BF16)] * PEER_GROUP
            + [pltpu.SMEM((PEER_GROUP, PEER_SLOTS), jnp.int32)] * 2
            + [pltpu.SemaphoreType.DMA((2,))])


def _chunk_mask():
    width = PEER_SLOTS * ROW_CHUNKS
    row = lax.broadcasted_iota(jnp.int32, (ROW_CHUNKS, width), 0)
    col = lax.broadcasted_iota(jnp.int32, (ROW_CHUNKS, width), 1)
    return (col & (ROW_CHUNKS - 1)) == row


def _stack_split(a):
    hi, lo = _split_bf16(a)
    return jnp.concatenate([hi, lo], axis=0)


def _to_chunk_major(x_ref, chunk_sc, tt):
    for c in range(ROW_CHUNKS):
        chunk_sc[pl.ds(c, tt, stride=ROW_CHUNKS), :] = x_ref[:, c * LANES:(c + 1) * LANES]


def _peer_dot_kernel(idx_ref, x_ref, tab_ref, gate_ref, fold_ref, act_ref, *scratch, tt):
    stages, (idx_a, idx_b, sems, x_sc, part_sc) = scratch[:PEER_GROUP], scratch[PEER_GROUP:]
    mask = _chunk_mask()
    _to_chunk_major(x_ref, x_sc, tt)

    def token(t, slot_ids, stage):
        rows = _gather_rows(slot_ids, tab_ref, stage)
        xt = x_sc[pl.ds(pl.multiple_of(t * ROW_CHUNKS, ROW_CHUNKS), ROW_CHUNKS), :]
        lhs = _stack_split(xt)
        r = lax.dot_general(lhs, rows, (((1,), (1,)), ((), ())), preferred_element_type=F32)
        r = r[0:ROW_CHUNKS] + r[ROW_CHUNKS:]
        part_sc[t] = jnp.where(mask, r, 0.0)

    _for_each_token(idx_ref, (idx_a, idx_b), sems, stages, token, tt)
    part = part_sc[...].reshape(tt * ROW_CHUNKS, PEER_SLOTS * ROW_CHUNKS)
    hi, lo = _split_bf16(part)
    fold = fold_ref[...]
    z = (jnp.dot(hi, fold, preferred_element_type=F32) + jnp.dot(lo, fold, preferred_element_type=F32))
    z = jnp.sum(z.reshape(tt, ROW_CHUNKS, PEER_SLOTS), axis=1)
    gelu = 0.5 * z * (1.0 + lax.erf(z * (1.0 / math.sqrt(2.0))))
    act_ref[0] = gelu * gate_ref[0]


def _peer_dot(idx_t, hn, tab, gate_t, fold, tt):
    nt = idx_t.shape[0]
    d = hn.shape[1]
    tile = lambda i: (i, 0, 0)
    slots = pl.BlockSpec((1, tt, PEER_SLOTS), tile)
    width = PEER_SLOTS * ROW_CHUNKS
    return pl.pallas_call(
        functools.partial(_peer_dot_kernel, tt=tt),
        grid=(nt,),
        in_specs=[slots,
                  pl.BlockSpec((tt, d), lambda i: (i, 0)),
                  pl.BlockSpec(memory_space=pltpu.VMEM),
                  slots,
                  pl.BlockSpec((width, PEER_SLOTS), lambda i: (0, 0))],
        out_specs=slots,
        out_shape=jax.ShapeDtypeStruct((nt, tt, PEER_SLOTS), F32),
        scratch_shapes=_gather_scratch()
                       + [pltpu.VMEM((tt * ROW_CHUNKS, LANES), F32), pltpu.VMEM((tt, ROW_CHUNKS, width), F32)],
        compiler_params=_cparams(("arbitrary",)),
        name="peer_dot",
    )(idx_t, hn, tab, gate_t, fold)


def _peer_out_kernel(idx_ref, act_ref, tab_ref, h_ref, spread_ref, y_ref, *scratch, tt):
    stages, (idx_a, idx_b, sems, y_sc, act_sc) = scratch[:PEER_GROUP], scratch[PEER_GROUP:]
    mask = _chunk_mask()
    hi, lo = _split_bf16(act_ref[0])
    spread = spread_ref[...]
    act_sc[...] = (jnp.dot(hi, spread, preferred_element_type=F32)
                   + jnp.dot(lo, spread, preferred_element_type=F32))

    def token(t, slot_ids, stage):
        rows = _gather_rows(slot_ids, tab_ref, stage)
        a = jnp.broadcast_to(act_sc[pl.ds(t, 1), :], mask.shape)
        lhs = _stack_split(jnp.where(mask, a, 0.0))
        o = jnp.dot(lhs, rows, preferred_element_type=F32)
        y_sc[pl.ds(pl.multiple_of(t * ROW_CHUNKS, ROW_CHUNKS), ROW_CHUNKS), :] = o[0:ROW_CHUNKS] + o[ROW_CHUNKS:]

    _for_each_token(idx_ref, (idx_a, idx_b), sems, stages, token, tt)
    for c in range(ROW_CHUNKS):
        cols = slice(c * LANES, (c + 1) * LANES)
        y_ref[:, cols] = h_ref[:, cols] + y_sc[pl.ds(c, tt, stride=ROW_CHUNKS), :]


def _peer_out(idx_t, act_t, tab, h, spread, tt):
    nt = idx_t.shape[0]
    tile = lambda i: (i, 0, 0)
    tok = pl.BlockSpec((tt, h.shape[1]), lambda i: (i, 0))
    width = PEER_SLOTS * ROW_CHUNKS
    return pl.pallas_call(
        functools.partial(_peer_out_kernel, tt=tt),
        grid=(nt,),
        in_specs=[pl.BlockSpec((1, tt, PEER_SLOTS), tile),
                  pl.BlockSpec((1, tt, PEER_SLOTS), tile),
                  pl.BlockSpec(memory_space=pltpu.VMEM), tok,
                  pl.BlockSpec((PEER_SLOTS, width), lambda i: (0, 0))],
        out_specs=tok,
        out_shape=jax.ShapeDtypeStruct(h.shape, F32),
        scratch_shapes=_gather_scratch()
                       + [pltpu.VMEM((tt * ROW_CHUNKS, LANES), F32), pltpu.VMEM((tt, width), F32)],
        compiler_params=_cparams(("arbitrary",)),
        name="peer_out",
    )(idx_t, act_t, tab, h, spread)


def _chunk_table(tab):
    n, d = tab.shape
    return tab.astype(BF16).reshape(n, d // LANES, LANES)


def _rope_tables(seq):
    pos = jnp.arange(seq, dtype=F32)
    inv_freq = ROPE_THETA ** (-jnp.arange(0, ROT_DIM, 2, dtype=F32) / ROT_DIM)
    ang = pos[:, None] * inv_freq[None, :]
    cos, sin = jnp.cos(ang), jnp.sin(ang)
    half = ROT_DIM // 2
    pad1 = jnp.ones((seq, HEAD_DIM - ROT_DIM), F32)
    pad0 = jnp.zeros((seq, HEAD_DIM - ROT_DIM), F32)
    zeros = jnp.zeros((seq, half), F32)
    cos_h = jnp.concatenate([cos, cos, pad1], axis=1)
    sa_h = jnp.concatenate([-sin, zeros, pad0], axis=1)
    sb_h = jnp.concatenate([zeros, sin, pad0], axis=1)
    two = lambda a: jnp.concatenate([a, a], axis=1)
    return two(cos_h), two(sa_h), two(sb_h)


def _layer(x2, batch, seq, attn_norm_g, w_in, q_norm_a, k_norm_a, out_norm_a, q_norm_b, k_norm_b,
           lambda_q1, lambda_k1, lambda_q2, lambda_k2, out_norm_b, w_out, ffn_norm_g, w_peer_q,
           peer_sub_keys, peer_u, peer_v, peer_tt=128):
    tile_heads = lambda g: jnp.tile(g.astype(F32), WIDTH_A // HEAD_DIM)[None, :]
    gains = jnp.concatenate([tile_heads(q_norm_a), tile_heads(k_norm_a),
                             tile_heads(q_norm_b), tile_heads(k_norm_b)], axis=0)
    grp = jnp.arange(WIDTH_A) // HEAD_DIM
    bd = (grp[:, None] == grp[None, :]).astype(BF16)
    cos_t, sa_t, sb_t = _rope_tables(seq)

    qa, ka, va, qbt, kb, vbt = _inproj(x2, attn_norm_g[None, :].astype(F32), w_in.astype(BF16), gains, bd,
                                       cos_t, sa_t, sb_t, batch, seq)
    outs = [_dilated(q, k, v) for q, k, v in zip(qa, ka, va)]
    lams = [v[None, :].astype(F32) for v in (lambda_q1, lambda_k1, lambda_q2, lambda_k2)]
    ob = _diffattn(qbt, kb, vbt, lams, out_norm_b[None, :].astype(F32))

    w_out_bf = w_out.astype(BF16)
    sk = peer_sub_keys.reshape(2 * PEER_HEADS, N_KEYS, KEY_DIM).astype(BF16)
    h, hn, scores_t = _outproj(x2, [o for o, _ in outs] + [l for _, l in outs], ob,
                               w_out_bf[:WIDTH_A], w_out_bf[WIDTH_A:], out_norm_a[None, :].astype(F32), bd,
                               ffn_norm_g[None, :].astype(F32), w_peer_q.astype(BF16), sk)
    idx_t, gate_t = _topk(scores_t, peer_tt)
    slot_of = jnp.arange(PEER_SLOTS * ROW_CHUNKS) // ROW_CHUNKS
    spread = (jnp.arange(PEER_SLOTS)[:, None] == slot_of[None, :]).astype(BF16)
    act_t = _peer_dot(idx_t, hn, _chunk_table(peer_u), gate_t, spread.T, peer_tt)
    return _peer_out(idx_t, act_t, _chunk_table(peer_v), h, spread, peer_tt)


def kernel(x, attn_norm_g, w_in, q_norm_a, k_norm_a, out_norm_a, q_norm_b, k_norm_b, lambda_q1, lambda_k1,
           lambda_q2, lambda_k2, out_norm_b, w_out, ffn_norm_g, w_peer_q, peer_sub_keys, peer_u, peer_v):
    batch, seq, d = x.shape
    y = _layer(x.reshape(batch * seq, d), batch, seq, attn_norm_g[0], w_in[0], q_norm_a[0], k_norm_a[0],
               out_norm_a[0], q_norm_b[0], k_norm_b[0], lambda_q1[0], lambda_k1[0], lambda_q2[0],
               lambda_k2[0], out_norm_b[0], w_out[0], ffn_norm_g[0], w_peer_q[0], peer_sub_keys[0],
               peer_u[0], peer_v[0])
    return y.reshape(batch, seq, d)
```

```python
import functools
import math

import jax
import jax.numpy as jnp
from jax import lax
from jax.experimental import pallas as pl
from jax.experimental.pallas import tpu as pltpu

F32 = jnp.float32
BF16 = jnp.bfloat16

HEAD_DIM = 64
N_HEADS_A = 8
WIDTH_A = N_HEADS_A * HEAD_DIM
DILATED_PATTERNS = ((128, 1), (512, 4), (2048, 16))
N_HEADS_B = 4
DIFF_DIM = HEAD_DIM
DIFF_V_DIM = 2 * DIFF_DIM
WIDTH_B = N_HEADS_B * DIFF_V_DIM
ROPE_THETA = 500000.0
ROT_DIM = HEAD_DIM // 4
BLOCK = 128
NORM_EPS = 1e-6
ATTN_SCALE = 1.0 / math.sqrt(HEAD_DIM)
LAM_INIT = 0.8 - 0.6 * math.exp(-0.3 * 0)

PEER_HEADS = 8
N_KEYS = 128
KEY_DIM = 128
PEER_TOPK = 16
PEER_SLOTS = PEER_HEADS * PEER_TOPK
ROW_CHUNKS = 8
PEER_GROUP = 8
PEER_ID_BUFS = 4

LANES = 128
VMEM_LIMIT = 56 * 1024 * 1024

NEG_INF = float("-inf")


def _cparams(sem):
    return pltpu.CompilerParams(dimension_semantics=sem, vmem_limit_bytes=VMEM_LIMIT)


def _split_bf16(a):
    hi = a.astype(BF16)
    lo = (a - hi.astype(F32)).astype(BF16)
    return hi, lo


def _group_sum(sq, bd):
    hi, lo = _split_bf16(sq)
    return (jnp.dot(hi, bd, preferred_element_type=F32)
            + jnp.dot(lo, bd, preferred_element_type=F32))


def _inproj_body(x_ref, g_ref, w_ref, gains_ref, bd_ref, cos_ref, sa_ref, sb_ref,
                 qa_refs, ka_refs, va_refs, qbt_ref, kb_ref, vbt_ref, stream_sc):
    x = x_ref[...]
    ms = jnp.mean(x * x, axis=-1, keepdims=True)
    xn = (x * lax.rsqrt(ms + NORM_EPS) * g_ref[...]).astype(BF16)
    proj = jnp.dot(xn, w_ref[...], preferred_element_type=F32)
    bd = bd_ref[...]
    rep = WIDTH_A // LANES
    cos = jnp.concatenate([cos_ref[...]] * rep, axis=1)
    sa = jnp.concatenate([sa_ref[...]] * rep, axis=1)
    sb = jnp.concatenate([sb_ref[...]] * rep, axis=1)
    half = ROT_DIM // 2

    def norm_rope(z, gain, scale):
        ssum = _group_sum(z * z, bd)
        zn = z * lax.rsqrt(ssum * (1.0 / HEAD_DIM) + NORM_EPS) * gain
        zr = (zn * cos + pltpu.roll(zn, WIDTH_A - half, 1) * sa + pltpu.roll(zn, half, 1) * sb)
        return zr * scale

    w = WIDTH_A
    tm = x.shape[0]
    mixer_a = (norm_rope(proj[:, 0:w], gains_ref[0:1, :], ATTN_SCALE),
               norm_rope(proj[:, w:2 * w], gains_ref[1:2, :], 1.0),
               proj[:, 2 * w:3 * w])
    for z, refs, sc in zip(mixer_a, (qa_refs, ka_refs, va_refs), stream_sc):
        nslab = w // LANES
        for k in range(nslab):
            sc[k] = z[:, k * LANES:(k + 1) * LANES]
        for (_, dil), ref in zip(DILATED_PATTERNS, refs):
            for r in range(dil):
                ref[r] = jnp.concatenate([sc[k, pl.ds(r, tm // dil, stride=dil), :] for k in range(nslab)],
                                         axis=1).astype(BF16)
    qbt_ref[...] = norm_rope(proj[:, 3 * w:4 * w], gains_ref[2:3, :], ATTN_SCALE).T.astype(BF16)
    kb_ref[...] = norm_rope(proj[:, 4 * w:5 * w], gains_ref[3:4, :], 1.0).astype(BF16)
    vbt_ref[...] = proj[:, 5 * w:6 * w].T.astype(BF16)


def _inproj_kernel(x_ref, g_ref, w_ref, gains_ref, bd_ref, cos_ref, sa_ref, sb_ref, *rest):
    nb = len(DILATED_PATTERNS)
    qa_refs, ka_refs, va_refs = rest[0:nb], rest[nb:2 * nb], rest[2 * nb:3 * nb]
    qbt_ref, kb_ref, vbt_ref = rest[3 * nb:3 * nb + 3]
    _inproj_body(x_ref, g_ref, w_ref, gains_ref, bd_ref, cos_ref, sa_ref, sb_ref,
                 qa_refs, ka_refs, va_refs, qbt_ref, kb_ref, vbt_ref, rest[3 * nb + 3:])


def _inproj(x2, g, w_bf, gains, bd, cos_t, sa_t, sb_t, batch, seq, tm=512):
    t, d = x2.shape
    ncol = w_bf.shape[1]
    nseq = seq // tm
    row = lambda i: (i, 0)
    fixed = lambda i: (0, 0)
    pos = lambda i: (i % nseq, 0)
    stream_specs, stream_sds = [], []
    for _, dil in DILATED_PATTERNS:
        stream_specs.append(pl.BlockSpec((None, dil, tm // dil, WIDTH_A), lambda i: (i // nseq, 0, i % nseq, 0)))
        stream_sds.append(jax.ShapeDtypeStruct((batch, dil, seq // dil, WIDTH_A), BF16))
    head_spec = pl.BlockSpec((None, WIDTH_B, tm), lambda i: (i // nseq, 0, i % nseq))
    head_sds = jax.ShapeDtypeStruct((batch, WIDTH_B, seq), BF16)
    outs = pl.pallas_call(
        _inproj_kernel,
        grid=(t // tm,),
        in_specs=[pl.BlockSpec((tm, d), row), pl.BlockSpec((1, d), fixed),
                  pl.BlockSpec((d, ncol), fixed), pl.BlockSpec((4, WIDTH_A), fixed),
                  pl.BlockSpec((WIDTH_A, WIDTH_A), fixed),
                  pl.BlockSpec((tm, LANES), pos), pl.BlockSpec((tm, LANES), pos),
                  pl.BlockSpec((tm, LANES), pos)],
        out_specs=stream_specs * 3 + [head_spec, pl.BlockSpec((tm, WIDTH_B), row), head_spec],
        out_shape=stream_sds * 3 + [head_sds, jax.ShapeDtypeStruct((t, WIDTH_B), BF16), head_sds],
        scratch_shapes=[pltpu.VMEM((WIDTH_A // LANES, tm, LANES), F32)] * 3,
        compiler_params=_cparams(("parallel",)),
        name="inproj",
    )(x2, g, w_bf, gains, bd, cos_t, sa_t, sb_t)
    nb = len(DILATED_PATTERNS)
    return outs[0:nb], outs[nb:2 * nb], outs[2 * nb:3 * nb], outs[3 * nb], outs[3 * nb + 1], outs[3 * nb + 2]


def _dilated_kernel(q_ref, kp_ref, k_ref, vp_ref, v_ref, o_ref, l_ref, kbuf, vbuf, *, tq):
    tile = pl.program_id(2)
    kbuf[0:BLOCK, :] = kp_ref[...]
    kbuf[BLOCK:, :] = k_ref[...]
    vbuf[0:BLOCK, :] = vp_ref[...]
    vbuf[BLOCK:, :] = v_ref[...]

    lane = lax.broadcasted_iota(jnp.int32, (BLOCK, LANES), 1)
    even = lane < HEAD_DIM
    head_masks = (even, lane >= HEAD_DIM)
    qi = lax.broadcasted_iota(jnp.int32, (BLOCK, 2 * BLOCK), 0)
    kj = lax.broadcasted_iota(jnp.int32, (BLOCK, 2 * BLOCK), 1)
    dist = qi + BLOCK - kj
    band = (dist >= 0) & (dist <= BLOCK)

    def body(blk, carry):
        r0 = pl.multiple_of(blk * BLOCK, BLOCK)
        first_key = jnp.where((tile == 0) & (blk == 0), BLOCK, 0)
        valid = band & (kj >= first_key)
        for pair in range(WIDTH_A // LANES):
            cols = slice(pair * LANES, (pair + 1) * LANES)
            q = q_ref[pl.ds(r0, BLOCK), cols]
            kk = kbuf[pl.ds(r0, 2 * BLOCK), cols]
            vv = vbuf[pl.ds(r0, 2 * BLOCK), cols]
            outs, lses = [], []
            for head_mask in head_masks:
                qh = jnp.where(head_mask, q, jnp.zeros_like(q))
                s = lax.dot_general(qh, kk, (((1,), (1,)), ((), ())), preferred_element_type=F32)
                s = jnp.where(valid, s, NEG_INF)
                m = jnp.max(s, axis=-1, keepdims=True)
                p = jnp.exp(s - m)
                den = jnp.sum(p, axis=-1, keepdims=True)
                o = jnp.dot(p.astype(BF16), vv, preferred_element_type=F32) / den
                outs.append(o)
                lses.append(jnp.broadcast_to(m + jnp.log(den), (BLOCK, LANES)))
            o_ref[pl.ds(r0, BLOCK), cols] = jnp.where(even, outs[0], outs[1])
            l_ref[pl.ds(r0, BLOCK), cols] = jnp.where(even, lses[0], lses[1])
        return carry

    lax.fori_loop(0, tq // BLOCK, body, 0)


def _dilated(qa, ka, va):
    batch, dilation, stream, _ = qa.shape
    tq = min(stream, 1024)
    nprev = tq // BLOCK
    cur = lambda b, r, i: (b, r, i, 0)
    prev = lambda b, r, i: (b, r, jnp.maximum(i * nprev - 1, 0), 0)
    blk_cur = pl.BlockSpec((None, None, tq, WIDTH_A), cur)
    blk_prev = pl.BlockSpec((None, None, BLOCK, WIDTH_A), prev)
    out_sds = jax.ShapeDtypeStruct(qa.shape, F32)
    return pl.pallas_call(
        functools.partial(_dilated_kernel, tq=tq),
        grid=(batch, dilation, stream // tq),
        in_specs=[blk_cur, blk_prev, blk_cur, blk_prev, blk_cur],
        out_specs=[blk_cur, blk_cur],
        out_shape=[out_sds, out_sds],
        scratch_shapes=[pltpu.VMEM((tq + BLOCK, WIDTH_A), BF16), pltpu.VMEM((tq + BLOCK, WIDTH_A), BF16)],
        compiler_params=_cparams(("parallel", "parallel", "arbitrary")),
        name=f"dilated_d{dilation}",
    )(qa, ka, ka, va, va)


def _diff_kernel(lq1_ref, lk1_ref, lq2_ref, lk2_ref, gain_ref, qt_ref, k_ref, vt_ref, o_ref,
                 m_sc, l_sc, acc_sc, *, tq):
    i = pl.program_id(2)
    heads = qt_ref.shape[0] // DIFF_V_DIM
    dim = lax.broadcasted_iota(jnp.int32, (DIFF_V_DIM, tq), 0)
    qs = []
    for hh in range(heads):
        qt = qt_ref[hh * DIFF_V_DIM:(hh + 1) * DIFF_V_DIM, :]
        zero = jnp.zeros_like(qt)
        qs += [jnp.where(dim < DIFF_DIM, qt, zero), jnp.where(dim >= DIFF_DIM, qt, zero)]

    m_sc[...] = jnp.full(m_sc.shape, NEG_INF, F32)
    l_sc[...] = jnp.zeros(l_sc.shape, F32)
    acc_sc[...] = jnp.zeros(acc_sc.shape, F32)

    key = lax.broadcasted_iota(jnp.int32, (tq, tq), 0)
    query = lax.broadcasted_iota(jnp.int32, (tq, tq), 1)
    causal = key <= query

    def step(j, masked):
        r0 = pl.multiple_of(j * tq, tq)
        for hh in range(heads):
            cols = slice(hh * DIFF_V_DIM, (hh + 1) * DIFF_V_DIM)
            kb = k_ref[pl.ds(r0, tq), cols]
            vtb = vt_ref[cols, pl.ds(r0, tq)]
            for c in range(2 * hh, 2 * hh + 2):
                s = jnp.dot(kb, qs[c], preferred_element_type=F32)
                if masked:
                    s = jnp.where(causal, s, NEG_INF)
                m_old = m_sc[c]
                m_new = jnp.maximum(m_old, jnp.max(s, axis=0, keepdims=True))
                alpha = jnp.exp(m_old - m_new)
                p = jnp.exp(s - m_new)
                l_sc[c] = alpha * l_sc[c] + jnp.sum(p, axis=0, keepdims=True)
                acc_sc[c] = alpha * acc_sc[c] + jnp.dot(vtb, p.astype(BF16), preferred_element_type=F32)
                m_sc[c] = m_new

    def body(j, carry):
        step(j, False)
        return carry

    lax.fori_loop(0, i, body, 0)
    step(i, True)

    lam = (jnp.exp(jnp.sum(lq1_ref[...] * lk1_ref[...], keepdims=True))
           - jnp.exp(jnp.sum(lq2_ref[...] * lk2_ref[...], keepdims=True)) + LAM_INIT)
    for hh in range(heads):
        cols = slice(hh * DIFF_V_DIM, (hh + 1) * DIFF_V_DIM)
        ot = acc_sc[2 * hh] / l_sc[2 * hh] - lam * (acc_sc[2 * hh + 1] / l_sc[2 * hh + 1])
        o = ot.T
        ms = jnp.mean(o * o, axis=-1, keepdims=True)
        o = o * lax.rsqrt(ms + NORM_EPS) * gain_ref[:, cols] * (1.0 - LAM_INIT)
        o_ref[:, cols] = o.astype(o_ref.dtype)


def _diffattn(qbt, kb, vbt, lams, gain_b, tq=512, heads=2):
    batch, _, seq = qbt.shape
    t = batch * seq
    nq = seq // tq
    width = heads * DIFF_V_DIM
    lam_spec = pl.BlockSpec((1, DIFF_DIM), lambda b, h, i: (0, 0))
    k_spec = pl.BlockSpec((seq, width), lambda b, h, i: (b, h))
    vt_spec = pl.BlockSpec((None, width, seq), lambda b, h, i: (b, h, 0))
    qt_spec = pl.BlockSpec((None, width, tq), lambda b, h, i: (b, h, i))
    o_spec = pl.BlockSpec((tq, width), lambda b, h, i: (b * nq + i, h))
    return pl.pallas_call(
        functools.partial(_diff_kernel, tq=tq),
        grid=(batch, N_HEADS_B // heads, nq),
        in_specs=[lam_spec] * 4 + [pl.BlockSpec((1, width), lambda b, h, i: (0, h)), qt_spec, k_spec, vt_spec],
        out_specs=o_spec,
        out_shape=jax.ShapeDtypeStruct((t, WIDTH_B), BF16),
        scratch_shapes=[pltpu.VMEM((2 * heads, 1, tq), F32), pltpu.VMEM((2 * heads, 1, tq), F32),
                        pltpu.VMEM((2 * heads, DIFF_V_DIM, tq), F32)],
        compiler_params=_cparams(("parallel", "parallel", "arbitrary")),
        name="diffattn",
    )(*lams, gain_b, qbt, kb, vbt)


def _outproj_kernel(x_ref, o1_ref, o2_ref, o3_ref, l1_ref, l2_ref, l3_ref, ob_ref, wa_ref, wb_ref,
                    gna_ref, bd_ref, gffn_ref, wq_ref, sk_ref, h_ref, hn_ref, sc_ref, *order_sc):
    tm = x_ref.shape[0]

    def token_order(ref, sc):
        dil = ref.shape[0]
        if dil == 1:
            return ref[0]
        nslab = WIDTH_A // LANES
        for r in range(dil):
            for k in range(nslab):
                sc[k, pl.ds(r, tm // dil, stride=dil), :] = ref[r, :, k * LANES:(k + 1) * LANES]
        return jnp.concatenate([sc[k] for k in range(nslab)], axis=1)

    o1, o2, o3, l1, l2, l3 = [token_order(ref, sc) for ref, sc in
                              zip((o1_ref, o2_ref, o3_ref, l1_ref, l2_ref, l3_ref), order_sc)]
    lmax = jnp.maximum(jnp.maximum(l1, l2), l3)
    e1, e2, e3 = jnp.exp(l1 - lmax), jnp.exp(l2 - lmax), jnp.exp(l3 - lmax)
    oa = (e1 * o1 + e2 * o2 + e3 * o3) / (e1 + e2 + e3)
    ssum = _group_sum(oa * oa, bd_ref[...])
    oa = oa * lax.rsqrt(ssum * (1.0 / HEAD_DIM) + NORM_EPS) * gna_ref[...]
    h = (x_ref[...]
         + jnp.dot(oa.astype(BF16), wa_ref[...], preferred_element_type=F32)
         + jnp.dot(ob_ref[...], wb_ref[...], preferred_element_type=F32))
    h_ref[...] = h
    ms = jnp.mean(h * h, axis=-1, keepdims=True)
    hn = h * lax.rsqrt(ms + NORM_EPS) * gffn_ref[...]
    hn_ref[...] = hn
    q = jnp.dot(hn.astype(BF16), wq_ref[...], preferred_element_type=F32).astype(BF16)
    for hc in range(2 * PEER_HEADS):
        qs = q[:, hc * KEY_DIM:(hc + 1) * KEY_DIM]
        sc_ref[hc] = lax.dot_general(sk_ref[hc], qs, (((1,), (1,)), ((), ())), preferred_element_type=F32)


def _outproj(x2, o_l, ob, wa, wb, gna, bd, gffn, wq, sk, tm=256):
    t, d = x2.shape
    nseq = (o_l[0].shape[1] * o_l[0].shape[2]) // tm
    row = lambda i: (i, 0)
    fixed = lambda i: (0, 0)
    half = pl.BlockSpec((tm, WIDTH_A), row)
    full = pl.BlockSpec((tm, d), row)
    streams = [pl.BlockSpec((None, a.shape[1], tm // a.shape[1], WIDTH_A),
                            lambda i: (i // nseq, 0, i % nseq, 0)) for a in o_l]
    nq = wq.shape[1]
    return pl.pallas_call(
        _outproj_kernel,
        grid=(t // tm,),
        in_specs=[full] + streams + [half]
                 + [pl.BlockSpec((WIDTH_A, d), fixed), pl.BlockSpec((WIDTH_B, d), fixed),
                    pl.BlockSpec((1, WIDTH_A), fixed), pl.BlockSpec((WIDTH_A, WIDTH_A), fixed),
                    pl.BlockSpec((1, d), fixed), pl.BlockSpec((d, nq), fixed),
                    pl.BlockSpec((2 * PEER_HEADS, N_KEYS, KEY_DIM), lambda i: (0, 0, 0))],
        out_specs=[full, full, pl.BlockSpec((2 * PEER_HEADS, N_KEYS, tm), lambda i: (0, 0, i))],
        out_shape=[jax.ShapeDtypeStruct((t, d), F32), jax.ShapeDtypeStruct((t, d), F32),
                   jax.ShapeDtypeStruct((2 * PEER_HEADS, N_KEYS, t), F32)],
        scratch_shapes=[pltpu.VMEM((WIDTH_A // LANES, tm, LANES), F32)] * len(o_l),
        compiler_params=_cparams(("parallel",)),
        name="outproj",
    )(x2, *o_l, ob, wa, wb, gna, bd, gffn, wq, sk)


def _extract_top(vals, payload, count):
    rows = vals.shape[0]
    iota = lax.broadcasted_iota(jnp.int32, vals.shape, 0).astype(F32)
    tops, picks = [], []
    for _ in range(count):
        m = jnp.max(vals, axis=0, keepdims=True)
        win = jnp.min(jnp.where(vals == m, iota, float(rows)), axis=0, keepdims=True)
        hit = iota == win
        tops.append(m)
        picks.append(win.astype(jnp.int32) if payload is None
                     else jnp.sum(jnp.where(hit, payload, 0), axis=0, keepdims=True))
        vals = jnp.where(hit, NEG_INF, vals)
    return tops, picks


PAIR_COUNTS = tuple(PEER_TOPK // (a + 1) for a in range(PEER_TOPK))
N_PAIRS = sum(PAIR_COUNTS)
PAIR_ROWS = -(-N_PAIRS // 8) * 8


def _topk_kernel(sc_ref, idx_ref, gate_ref, idx_sc, gate_sc, cand_sc, cid_sc):
    tokens = cand_sc.shape[1]
    cand_sc[N_PAIRS:, :] = jnp.full((PAIR_ROWS - N_PAIRS, tokens), NEG_INF, F32)
    cid_sc[N_PAIRS:, :] = jnp.zeros((PAIR_ROWS - N_PAIRS, tokens), jnp.int32)
    for h in range(PEER_HEADS):
        v1, i1 = _extract_top(sc_ref[2 * h], None, PEER_TOPK)
        v2, i2 = _extract_top(sc_ref[2 * h + 1], None, PEER_TOPK)
        v2m = jnp.concatenate(v2, axis=0)
        i2m = jnp.concatenate(i2, axis=0)
        off = 0
        for a, nb in enumerate(PAIR_COUNTS):
            cand_sc[off:off + nb, :] = v1[a] + v2m[0:nb]
            cid_sc[off:off + nb, :] = i1[a] * N_KEYS + i2m[0:nb]
            off += nb
        top, eid = _extract_top(cand_sc[...], cid_sc[...], PEER_TOPK)
        ex = [jnp.exp(tv - top[0]) for tv in top]
        den = ex[0]
        for e in ex[1:]:
            den = den + e
        for k in range(PEER_TOPK):
            j = h * PEER_TOPK + k
            idx_sc[j:j + 1, :] = eid[k]
            gate_sc[j:j + 1, :] = ex[k] / den
    idx_ref[0] = idx_sc[...].T
    gate_ref[0] = gate_sc[...].T


def _topk(scores_t, tt):
    t = scores_t.shape[-1]
    nt = t // tt
    out_spec = pl.BlockSpec((1, tt, PEER_SLOTS), lambda i: (i, 0, 0))
    return pl.pallas_call(
        _topk_kernel,
        grid=(nt,),
        in_specs=[pl.BlockSpec((2 * PEER_HEADS, N_KEYS, tt), lambda i: (0, 0, i))],
        out_specs=[out_spec, out_spec],
        out_shape=[jax.ShapeDtypeStruct((nt, tt, PEER_SLOTS), jnp.int32),
                   jax.ShapeDtypeStruct((nt, tt, PEER_SLOTS), F32)],
        scratch_shapes=[pltpu.VMEM((PEER_SLOTS, tt), jnp.int32), pltpu.VMEM((PEER_SLOTS, tt), F32),
                        pltpu.VMEM((PAIR_ROWS, tt), F32), pltpu.VMEM((PAIR_ROWS, tt), jnp.int32)],
        compiler_params=_cparams(("parallel",)),
        name="topk",
    )(scores_t)


def _gather_rows(slot_ids, tab_ref, stage):
    for j in range(PEER_SLOTS):
        stage[j * ROW_CHUNKS:(j + 1) * ROW_CHUNKS, :] = tab_ref[slot_ids[j]]
    return stage[...]


def _for_each_token(idx_ref, idx_bufs, sems, stages, token_fn, tt):
    group = len(stages)
    nbuf = len(idx_bufs)
    ngroups = tt // group
    ahead = nbuf - 1

    def ids_copy(g, k):
        return pltpu.make_async_copy(idx_ref.at[0, pl.ds(g * group, group)], idx_bufs[k], sems.at[k])

    for g in range(ahead):
        ids_copy(g, g).start()

    def round_(q, carry):
        for k in range(nbuf):
            g = q * nbuf + k
            ids_copy(g, k).wait()

            @pl.when(g + ahead < ngroups)
            def _():
                ids_copy(g + ahead, (k + ahead) % nbuf).start()

            for u, stage in enumerate(stages):
                token_fn(g * group + u, idx_bufs[k].at[u], stage)
        return carry

    lax.fori_loop(0, ngroups // nbuf, round_, 0)


def _gather_scratch():
    return ([pltpu.VMEM((PEER_SLOTS * ROW_CHUNKS, LANES), BF16)] * PEER_GROUP
            + [pltpu.SMEM((PEER_GROUP, PEER_SLOTS), jnp.int32)] * PEER_ID_BUFS
            + [pltpu.SemaphoreType.DMA((PEER_ID_BUFS,))])


def _split_gather_scratch(scratch):
    nid = PEER_GROUP + PEER_ID_BUFS
    return scratch[:PEER_GROUP], scratch[PEER_GROUP:nid], scratch[nid], scratch[nid + 1:]


def _chunk_mask():
    width = PEER_SLOTS * ROW_CHUNKS
    row = lax.broadcasted_iota(jnp.int32, (ROW_CHUNKS, width), 0)
    col = lax.broadcasted_iota(jnp.int32, (ROW_CHUNKS, width), 1)
    return (col & (ROW_CHUNKS - 1)) == row


def _stack_split(a):
    hi, lo = _split_bf16(a)
    return jnp.concatenate([hi, lo], axis=0)


def _to_chunk_major(x_ref, chunk_sc, tt):
    for c in range(ROW_CHUNKS):
        chunk_sc[pl.ds(c, tt, stride=ROW_CHUNKS), :] = x_ref[:, c * LANES:(c + 1) * LANES]


def _peer_dot_kernel(idx_ref, x_ref, tab_ref, gate_ref, fold_ref, act_ref, *scratch, tt):
    stages, idx_bufs, sems, (x_sc, part_sc) = _split_gather_scratch(scratch)
    mask = _chunk_mask()
    _to_chunk_major(x_ref, x_sc, tt)

    def token(t, slot_ids, stage):
        rows = _gather_rows(slot_ids, tab_ref, stage)
        xt = x_sc[pl.ds(pl.multiple_of(t * ROW_CHUNKS, ROW_CHUNKS), ROW_CHUNKS), :]
        lhs = _stack_split(xt)
        r = lax.dot_general(lhs, rows, (((1,), (1,)), ((), ())), preferred_element_type=F32)
        r = r[0:ROW_CHUNKS] + r[ROW_CHUNKS:]
        part_sc[t] = jnp.where(mask, r, 0.0)

    _for_each_token(idx_ref, idx_bufs, sems, stages, token, tt)
    part = part_sc[...].reshape(tt * ROW_CHUNKS, PEER_SLOTS * ROW_CHUNKS)
    hi, lo = _split_bf16(part)
    fold = fold_ref[...]
    z = (jnp.dot(hi, fold, preferred_element_type=F32) + jnp.dot(lo, fold, preferred_element_type=F32))
    z = jnp.sum(z.reshape(tt, ROW_CHUNKS, PEER_SLOTS), axis=1)
    gelu = 0.5 * z * (1.0 + lax.erf(z * (1.0 / math.sqrt(2.0))))
    act_ref[0] = gelu * gate_ref[0]


def _peer_dot(idx_t, hn, tab, gate_t, fold, tt):
    nt = idx_t.shape[0]
    d = hn.shape[1]
    tile = lambda i: (i, 0, 0)
    slots = pl.BlockSpec((1, tt, PEER_SLOTS), tile)
    width = PEER_SLOTS * ROW_CHUNKS
    return pl.pallas_call(
        functools.partial(_peer_dot_kernel, tt=tt),
        grid=(nt,),
        in_specs=[slots,
                  pl.BlockSpec((tt, d), lambda i: (i, 0)),
                  pl.BlockSpec(memory_space=pltpu.VMEM),
                  slots,
                  pl.BlockSpec((width, PEER_SLOTS), lambda i: (0, 0))],
        out_specs=slots,
        out_shape=jax.ShapeDtypeStruct((nt, tt, PEER_SLOTS), F32),
        scratch_shapes=_gather_scratch()
                       + [pltpu.VMEM((tt * ROW_CHUNKS, LANES), F32), pltpu.VMEM((tt, ROW_CHUNKS, width), F32)],
        compiler_params=_cparams(("arbitrary",)),
        name="peer_dot",
    )(idx_t, hn, tab, gate_t, fold)


def _peer_out_kernel(idx_ref, act_ref, tab_ref, h_ref, spread_ref, y_ref, *scratch, tt):
    stages, idx_bufs, sems, (y_sc, act_sc) = _split_gather_scratch(scratch)
    mask = _chunk_mask()
    hi, lo = _split_bf16(act_ref[0])
    spread = spread_ref[...]
    act_sc[...] = (jnp.dot(hi, spread, preferred_element_type=F32)
                   + jnp.dot(lo, spread, preferred_element_type=F32))

    def token(t, slot_ids, stage):
        rows = _gather_rows(slot_ids, tab_ref, stage)
        a = jnp.broadcast_to(act_sc[pl.ds(t, 1), :], mask.shape)
        lhs = _stack_split(jnp.where(mask, a, 0.0))
        o = jnp.dot(lhs, rows, preferred_element_type=F32)
        y_sc[pl.ds(pl.multiple_of(t * ROW_CHUNKS, ROW_CHUNKS), ROW_CHUNKS), :] = o[0:ROW_CHUNKS] + o[ROW_CHUNKS:]

    _for_each_token(idx_ref, idx_bufs, sems, stages, token, tt)
    for c in range(ROW_CHUNKS):
        cols = slice(c * LANES, (c + 1) * LANES)
        y_ref[:, cols] = h_ref[:, cols] + y_sc[pl.ds(c, tt, stride=ROW_CHUNKS), :]


def _peer_out(idx_t, act_t, tab, h, spread, tt):
    nt = idx_t.shape[0]
    tile = lambda i: (i, 0, 0)
    tok = pl.BlockSpec((tt, h.shape[1]), lambda i: (i, 0))
    width = PEER_SLOTS * ROW_CHUNKS
    return pl.pallas_call(
        functools.partial(_peer_out_kernel, tt=tt),
        grid=(nt,),
        in_specs=[pl.BlockSpec((1, tt, PEER_SLOTS), tile),
                  pl.BlockSpec((1, tt, PEER_SLOTS), tile),
                  pl.BlockSpec(memory_space=pltpu.VMEM), tok,
                  pl.BlockSpec((PEER_SLOTS, width), lambda i: (0, 0))],
        out_specs=tok,
        out_shape=jax.ShapeDtypeStruct(h.shape, F32),
        scratch_shapes=_gather_scratch()
                       + [pltpu.VMEM((tt * ROW_CHUNKS, LANES), F32), pltpu.VMEM((tt, width), F32)],
        compiler_params=_cparams(("arbitrary",)),
        name="peer_out",
    )(idx_t, act_t, tab, h, spread)


def _chunk_table(tab):
    n, d = tab.shape
    return tab.astype(BF16).reshape(n, d // LANES, LANES)


def _rope_tables(seq):
    pos = jnp.arange(seq, dtype=F32)
    inv_freq = ROPE_THETA ** (-jnp.arange(0, ROT_DIM, 2, dtype=F32) / ROT_DIM)
    ang = pos[:, None] * inv_freq[None, :]
    cos, sin = jnp.cos(ang), jnp.sin(ang)
    half = ROT_DIM // 2
    pad1 = jnp.ones((seq, HEAD_DIM - ROT_DIM), F32)
    pad0 = jnp.zeros((seq, HEAD_DIM - ROT_DIM), F32)
    zeros = jnp.zeros((seq, half), F32)
    cos_h = jnp.concatenate([cos, cos, pad1], axis=1)
    sa_h = jnp.concatenate([-sin, zeros, pad0], axis=1)
    sb_h = jnp.concatenate([zeros, sin, pad0], axis=1)
    two = lambda a: jnp.concatenate([a, a], axis=1)
    return two(cos_h), two(sa_h), two(sb_h)


def _layer(x2, batch, seq, attn_norm_g, w_in, q_norm_a, k_norm_a, out_norm_a, q_norm_b, k_norm_b,
           lambda_q1, lambda_k1, lambda_q2, lambda_k2, out_norm_b, w_out, ffn_norm_g, w_peer_q,
           peer_sub_keys, peer_u, peer_v, peer_tt=128):
    tile_heads = lambda g: jnp.tile(g.astype(F32), WIDTH_A // HEAD_DIM)[None, :]
    gains = jnp.concatenate([tile_heads(q_norm_a), tile_heads(k_norm_a),
                             tile_heads(q_norm_b), tile_heads(k_norm_b)], axis=0)
    grp = jnp.arange(WIDTH_A) // HEAD_DIM
    bd = (grp[:, None] == grp[None, :]).astype(BF16)
    cos_t, sa_t, sb_t = _rope_tables(seq)

    qa, ka, va, qbt, kb, vbt = _inproj(x2, attn_norm_g[None, :].astype(F32), w_in.astype(BF16), gains, bd,
                                       cos_t, sa_t, sb_t, batch, seq)
    outs = [_dilated(q, k, v) for q, k, v in zip(qa, ka, va)]
    lams = [v[None, :].astype(F32) for v in (lambda_q1, lambda_k1, lambda_q2, lambda_k2)]
    ob = _diffattn(qbt, kb, vbt, lams, out_norm_b[None, :].astype(F32))

    w_out_bf = w_out.astype(BF16)
    sk = peer_sub_keys.reshape(2 * PEER_HEADS, N_KEYS, KEY_DIM).astype(BF16)
    h, hn, scores_t = _outproj(x2, [o for o, _ in outs] + [l for _, l in outs], ob,
                               w_out_bf[:WIDTH_A], w_out_bf[WIDTH_A:], out_norm_a[None, :].astype(F32), bd,
                               ffn_norm_g[None, :].astype(F32), w_peer_q.astype(BF16), sk)
    idx_t, gate_t = _topk(scores_t, peer_tt)
    slot_of = jnp.arange(PEER_SLOTS * ROW_CHUNKS) // ROW_CHUNKS
    spread = (jnp.arange(PEER_SLOTS)[:, None] == slot_of[None, :]).astype(BF16)
    act_t = _peer_dot(idx_t, hn, _chunk_table(peer_u), gate_t, spread.T, peer_tt)
    return _peer_out(idx_t, act_t, _chunk_table(peer_v), h, spread, peer_tt)


def kernel(x, attn_norm_g, w_in, q_norm_a, k_norm_a, out_norm_a, q_norm_b, k_norm_b, lambda_q1, lambda_k1,
           lambda_q2, lambda_k2, out_norm_b, w_out, ffn_norm_g, w_peer_q, peer_sub_keys, peer_u, peer_v):
    batch, seq, d = x.shape
    y = _layer(x.reshape(batch * seq, d), batch, seq, attn_norm_g[0], w_in[0], q_norm_a[0], k_norm_a[0],
               out_norm_a[0], q_norm_b[0], k_norm_b[0], lambda_q1[0], lambda_k1[0], lambda_q2[0],
               lambda_k2[0], out_norm_b[0], w_out[0], ffn_norm_g[0], w_peer_q[0], peer_sub_keys[0],
               peer_u[0], peer_v[0])
    return y.reshape(batch, seq, d)
```

```python
import functools
import math

import jax
import jax.numpy as jnp
from jax import lax
from jax.experimental import pallas as pl
from jax.experimental.pallas import tpu as pltpu

F32 = jnp.float32
BF16 = jnp.bfloat16

HEAD_DIM = 64
N_HEADS_A = 8
WIDTH_A = N_HEADS_A * HEAD_DIM
DILATED_PATTERNS = ((128, 1), (512, 4), (2048, 16))
N_HEADS_B = 4
DIFF_DIM = HEAD_DIM
DIFF_V_DIM = 2 * DIFF_DIM
WIDTH_B = N_HEADS_B * DIFF_V_DIM
ROPE_THETA = 500000.0
ROT_DIM = HEAD_DIM // 4
BLOCK = 128
NORM_EPS = 1e-6
ATTN_SCALE = 1.0 / math.sqrt(HEAD_DIM)
LAM_INIT = 0.8 - 0.6 * math.exp(-0.3 * 0)

PEER_HEADS = 8
N_KEYS = 128
KEY_DIM = 128
PEER_TOPK = 16
PEER_SLOTS = PEER_HEADS * PEER_TOPK
ROW_CHUNKS = 8
PEER_GROUP = 8
PEER_ID_BUFS = 4

LANES = 128
VMEM_LIMIT = 56 * 1024 * 1024

NEG_INF = float("-inf")


def _cparams(sem):
    return pltpu.CompilerParams(dimension_semantics=sem, vmem_limit_bytes=VMEM_LIMIT)


def _split_bf16(a):
    hi = a.astype(BF16)
    lo = (a - hi.astype(F32)).astype(BF16)
    return hi, lo


def _group_sum(sq, bd):
    hi, lo = _split_bf16(sq)
    return (jnp.dot(hi, bd, preferred_element_type=F32)
            + jnp.dot(lo, bd, preferred_element_type=F32))


def _inproj_body(x_ref, g_ref, w_ref, gains_ref, bd_ref, cos_ref, sa_ref, sb_ref,
                 qa_refs, ka_refs, va_refs, qbt_ref, kb_ref, vbt_ref, stream_sc):
    x = x_ref[...]
    ms = jnp.mean(x * x, axis=-1, keepdims=True)
    xn = (x * lax.rsqrt(ms + NORM_EPS) * g_ref[...]).astype(BF16)
    proj = jnp.dot(xn, w_ref[...], preferred_element_type=F32)
    bd = bd_ref[...]
    rep = WIDTH_A // LANES
    cos = jnp.concatenate([cos_ref[...]] * rep, axis=1)
    sa = jnp.concatenate([sa_ref[...]] * rep, axis=1)
    sb = jnp.concatenate([sb_ref[...]] * rep, axis=1)
    half = ROT_DIM // 2

    def norm_rope(z, gain, scale):
        ssum = _group_sum(z * z, bd)
        zn = z * lax.rsqrt(ssum * (1.0 / HEAD_DIM) + NORM_EPS) * gain
        zr = (zn * cos + pltpu.roll(zn, WIDTH_A - half, 1) * sa + pltpu.roll(zn, half, 1) * sb)
        return zr * scale

    w = WIDTH_A
    tm = x.shape[0]
    mixer_a = (norm_rope(proj[:, 0:w], gains_ref[0:1, :], ATTN_SCALE),
               norm_rope(proj[:, w:2 * w], gains_ref[1:2, :], 1.0),
               proj[:, 2 * w:3 * w])
    for z, refs, sc in zip(mixer_a, (qa_refs, ka_refs, va_refs), stream_sc):
        nslab = w // LANES
        for k in range(nslab):
            sc[k] = z[:, k * LANES:(k + 1) * LANES]
        for (_, dil), ref in zip(DILATED_PATTERNS, refs):
            for r in range(dil):
                ref[r] = jnp.concatenate([sc[k, pl.ds(r, tm // dil, stride=dil), :] for k in range(nslab)],
                                         axis=1).astype(BF16)
    qbt_ref[...] = norm_rope(proj[:, 3 * w:4 * w], gains_ref[2:3, :], ATTN_SCALE).T.astype(BF16)
    kb_ref[...] = norm_rope(proj[:, 4 * w:5 * w], gains_ref[3:4, :], 1.0).astype(BF16)
    vbt_ref[...] = proj[:, 5 * w:6 * w].T.astype(BF16)


def _inproj_kernel(x_ref, g_ref, w_ref, gains_ref, bd_ref, cos_ref, sa_ref, sb_ref, *rest):
    nb = len(DILATED_PATTERNS)
    qa_refs, ka_refs, va_refs = rest[0:nb], rest[nb:2 * nb], rest[2 * nb:3 * nb]
    qbt_ref, kb_ref, vbt_ref = rest[3 * nb:3 * nb + 3]
    _inproj_body(x_ref, g_ref, w_ref, gains_ref, bd_ref, cos_ref, sa_ref, sb_ref,
                 qa_refs, ka_refs, va_refs, qbt_ref, kb_ref, vbt_ref, rest[3 * nb + 3:])


def _inproj(x2, g, w_bf, gains, bd, cos_t, sa_t, sb_t, batch, seq, tm=512):
    t, d = x2.shape
    ncol = w_bf.shape[1]
    nseq = seq // tm
    row = lambda i: (i, 0)
    fixed = lambda i: (0, 0)
    pos = lambda i: (i % nseq, 0)
    stream_specs, stream_sds = [], []
    for _, dil in DILATED_PATTERNS:
        stream_specs.append(pl.BlockSpec((None, dil, tm // dil, WIDTH_A), lambda i: (i // nseq, 0, i % nseq, 0)))
        stream_sds.append(jax.ShapeDtypeStruct((batch, dil, seq // dil, WIDTH_A), BF16))
    head_spec = pl.BlockSpec((None, WIDTH_B, tm), lambda i: (i // nseq, 0, i % nseq))
    head_sds = jax.ShapeDtypeStruct((batch, WIDTH_B, seq), BF16)
    outs = pl.pallas_call(
        _inproj_kernel,
        grid=(t // tm,),
        in_specs=[pl.BlockSpec((tm, d), row), pl.BlockSpec((1, d), fixed),
                  pl.BlockSpec((d, ncol), fixed), pl.BlockSpec((4, WIDTH_A), fixed),
                  pl.BlockSpec((WIDTH_A, WIDTH_A), fixed),
                  pl.BlockSpec((tm, LANES), pos), pl.BlockSpec((tm, LANES), pos),
                  pl.BlockSpec((tm, LANES), pos)],
        out_specs=stream_specs * 3 + [head_spec, pl.BlockSpec((tm, WIDTH_B), row), head_spec],
        out_shape=stream_sds * 3 + [head_sds, jax.ShapeDtypeStruct((t, WIDTH_B), BF16), head_sds],
        scratch_shapes=[pltpu.VMEM((WIDTH_A // LANES, tm, LANES), F32)] * 3,
        compiler_params=_cparams(("parallel",)),
        name="inproj",
    )(x2, g, w_bf, gains, bd, cos_t, sa_t, sb_t)
    nb = len(DILATED_PATTERNS)
    return outs[0:nb], outs[nb:2 * nb], outs[2 * nb:3 * nb], outs[3 * nb], outs[3 * nb + 1], outs[3 * nb + 2]


def _dilated_kernel(q_ref, kp_ref, k_ref, vp_ref, v_ref, o_ref, l_ref, kbuf, vbuf, *, tq):
    tile = pl.program_id(2)
    kbuf[0:BLOCK, :] = kp_ref[...]
    kbuf[BLOCK:, :] = k_ref[...]
    vbuf[0:BLOCK, :] = vp_ref[...]
    vbuf[BLOCK:, :] = v_ref[...]

    lane = lax.broadcasted_iota(jnp.int32, (BLOCK, LANES), 1)
    even = lane < HEAD_DIM
    head_masks = (even, lane >= HEAD_DIM)
    qi = lax.broadcasted_iota(jnp.int32, (BLOCK, 2 * BLOCK), 0)
    kj = lax.broadcasted_iota(jnp.int32, (BLOCK, 2 * BLOCK), 1)
    dist = qi + BLOCK - kj
    band = (dist >= 0) & (dist <= BLOCK)

    def body(blk, carry):
        r0 = pl.multiple_of(blk * BLOCK, BLOCK)
        first_key = jnp.where((tile == 0) & (blk == 0), BLOCK, 0)
        valid = band & (kj >= first_key)
        for pair in range(WIDTH_A // LANES):
            cols = slice(pair * LANES, (pair + 1) * LANES)
            q = q_ref[pl.ds(r0, BLOCK), cols]
            kk = kbuf[pl.ds(r0, 2 * BLOCK), cols]
            vv = vbuf[pl.ds(r0, 2 * BLOCK), cols]
            outs, lses = [], []
            for head_mask in head_masks:
                qh = jnp.where(head_mask, q, jnp.zeros_like(q))
                s = lax.dot_general(qh, kk, (((1,), (1,)), ((), ())), preferred_element_type=F32)
                s = jnp.where(valid, s, NEG_INF)
                m = jnp.max(s, axis=-1, keepdims=True)
                p = jnp.exp(s - m)
                den = jnp.sum(p, axis=-1, keepdims=True)
                o = jnp.dot(p.astype(BF16), vv, preferred_element_type=F32) / den
                outs.append(o)
                lses.append(jnp.broadcast_to(m + jnp.log(den), (BLOCK, LANES)))
            o_ref[pl.ds(r0, BLOCK), cols] = jnp.where(even, outs[0], outs[1])
            l_ref[pl.ds(r0, BLOCK), cols] = jnp.where(even, lses[0], lses[1])
        return carry

    lax.fori_loop(0, tq // BLOCK, body, 0)


def _dilated(qa, ka, va):
    batch, dilation, stream, _ = qa.shape
    tq = min(stream, 1024)
    nprev = tq // BLOCK
    cur = lambda b, r, i: (b, r, i, 0)
    prev = lambda b, r, i: (b, r, jnp.maximum(i * nprev - 1, 0), 0)
    blk_cur = pl.BlockSpec((None, None, tq, WIDTH_A), cur)
    blk_prev = pl.BlockSpec((None, None, BLOCK, WIDTH_A), prev)
    out_sds = jax.ShapeDtypeStruct(qa.shape, F32)
    return pl.pallas_call(
        functools.partial(_dilated_kernel, tq=tq),
        grid=(batch, dilation, stream // tq),
        in_specs=[blk_cur, blk_prev, blk_cur, blk_prev, blk_cur],
        out_specs=[blk_cur, blk_cur],
        out_shape=[out_sds, out_sds],
        scratch_shapes=[pltpu.VMEM((tq + BLOCK, WIDTH_A), BF16), pltpu.VMEM((tq + BLOCK, WIDTH_A), BF16)],
        compiler_params=_cparams(("parallel", "parallel", "arbitrary")),
        name=f"dilated_d{dilation}",
    )(qa, ka, ka, va, va)


def _diff_kernel(lq1_ref, lk1_ref, lq2_ref, lk2_ref, gain_ref, qt_ref, k_ref, vt_ref, o_ref,
                 m_sc, l_sc, acc_sc, *, tq):
    i = pl.program_id(2)
    heads = qt_ref.shape[0] // DIFF_V_DIM
    dim = lax.broadcasted_iota(jnp.int32, (DIFF_V_DIM, tq), 0)
    qs = []
    for hh in range(heads):
        qt = qt_ref[hh * DIFF_V_DIM:(hh + 1) * DIFF_V_DIM, :]
        zero = jnp.zeros_like(qt)
        qs += [jnp.where(dim < DIFF_DIM, qt, zero), jnp.where(dim >= DIFF_DIM, qt, zero)]

    m_sc[...] = jnp.full(m_sc.shape, NEG_INF, F32)
    l_sc[...] = jnp.zeros(l_sc.shape, F32)
    acc_sc[...] = jnp.zeros(acc_sc.shape, F32)

    key = lax.broadcasted_iota(jnp.int32, (tq, tq), 0)
    query = lax.broadcasted_iota(jnp.int32, (tq, tq), 1)
    causal = key <= query

    def step(j, masked):
        r0 = pl.multiple_of(j * tq, tq)
        for hh in range(heads):
            cols = slice(hh * DIFF_V_DIM, (hh + 1) * DIFF_V_DIM)
            kb = k_ref[pl.ds(r0, tq), cols]
            vtb = vt_ref[cols, pl.ds(r0, tq)]
            for c in range(2 * hh, 2 * hh + 2):
                s = jnp.dot(kb, qs[c], preferred_element_type=F32)
                if masked:
                    s = jnp.where(causal, s, NEG_INF)
                m_old = m_sc[c]
                m_new = jnp.maximum(m_old, jnp.max(s, axis=0, keepdims=True))
                alpha = jnp.exp(m_old - m_new)
                p = jnp.exp(s - m_new)
                l_sc[c] = alpha * l_sc[c] + jnp.sum(p, axis=0, keepdims=True)
                acc_sc[c] = alpha * acc_sc[c] + jnp.dot(vtb, p.astype(BF16), preferred_element_type=F32)
                m_sc[c] = m_new

    def body(j, carry):
        step(j, False)
        return carry

    lax.fori_loop(0, i, body, 0)
    step(i, True)

    lam = (jnp.exp(jnp.sum(lq1_ref[...] * lk1_ref[...], keepdims=True))
           - jnp.exp(jnp.sum(lq2_ref[...] * lk2_ref[...], keepdims=True)) + LAM_INIT)
    for hh in range(heads):
        cols = slice(hh * DIFF_V_DIM, (hh + 1) * DIFF_V_DIM)
        ot = acc_sc[2 * hh] / l_sc[2 * hh] - lam * (acc_sc[2 * hh + 1] / l_sc[2 * hh + 1])
        o = ot.T
        ms = jnp.mean(o * o, axis=-1, keepdims=True)
        o = o * lax.rsqrt(ms + NORM_EPS) * gain_ref[:, cols] * (1.0 - LAM_INIT)
        o_ref[:, cols] = o.astype(o_ref.dtype)


def _diffattn(qbt, kb, vbt, lams, gain_b, tq=512, heads=2):
    batch, _, seq = qbt.shape
    t = batch * seq
    nq = seq // tq
    width = heads * DIFF_V_DIM
    lam_spec = pl.BlockSpec((1, DIFF_DIM), lambda b, h, i: (0, 0))
    k_spec = pl.BlockSpec((seq, width), lambda b, h, i: (b, h))
    vt_spec = pl.BlockSpec((None, width, seq), lambda b, h, i: (b, h, 0))
    qt_spec = pl.BlockSpec((None, width, tq), lambda b, h, i: (b, h, i))
    o_spec = pl.BlockSpec((tq, width), lambda b, h, i: (b * nq + i, h))
    return pl.pallas_call(
        functools.partial(_diff_kernel, tq=tq),
        grid=(batch, N_HEADS_B // heads, nq),
        in_specs=[lam_spec] * 4 + [pl.BlockSpec((1, width), lambda b, h, i: (0, h)), qt_spec, k_spec, vt_spec],
        out_specs=o_spec,
        out_shape=jax.ShapeDtypeStruct((t, WIDTH_B), BF16),
        scratch_shapes=[pltpu.VMEM((2 * heads, 1, tq), F32), pltpu.VMEM((2 * heads, 1, tq), F32),
                        pltpu.VMEM((2 * heads, DIFF_V_DIM, tq), F32)],
        compiler_params=_cparams(("parallel", "parallel", "arbitrary")),
        name="diffattn",
    )(*lams, gain_b, qbt, kb, vbt)


def _outproj_kernel(x_ref, o1_ref, o2_ref, o3_ref, l1_ref, l2_ref, l3_ref, ob_ref, wa_ref, wb_ref,
                    gna_ref, bd_ref, gffn_ref, wq_ref, sk_ref, h_ref, hn_ref, sc_ref, *order_sc):
    tm = x_ref.shape[0]

    def token_order(ref, sc):
        dil = ref.shape[0]
        if dil == 1:
            return ref[0]
        nslab = WIDTH_A // LANES
        for r in range(dil):
            for k in range(nslab):
                sc[k, pl.ds(r, tm // dil, stride=dil), :] = ref[r, :, k * LANES:(k + 1) * LANES]
        return jnp.concatenate([sc[k] for k in range(nslab)], axis=1)

    o1, o2, o3, l1, l2, l3 = [token_order(ref, sc) for ref, sc in
                              zip((o1_ref, o2_ref, o3_ref, l1_ref, l2_ref, l3_ref), order_sc)]
    lmax = jnp.maximum(jnp.maximum(l1, l2), l3)
    e1, e2, e3 = jnp.exp(l1 - lmax), jnp.exp(l2 - lmax), jnp.exp(l3 - lmax)
    oa = (e1 * o1 + e2 * o2 + e3 * o3) / (e1 + e2 + e3)
    ssum = _group_sum(oa * oa, bd_ref[...])
    oa = oa * lax.rsqrt(ssum * (1.0 / HEAD_DIM) + NORM_EPS) * gna_ref[...]
    h = (x_ref[...]
         + jnp.dot(oa.astype(BF16), wa_ref[...], preferred_element_type=F32)
         + jnp.dot(ob_ref[...], wb_ref[...], preferred_element_type=F32))
    h_ref[...] = h
    ms = jnp.mean(h * h, axis=-1, keepdims=True)
    hn = h * lax.rsqrt(ms + NORM_EPS) * gffn_ref[...]
    hn_ref[...] = hn
    q = jnp.dot(hn.astype(BF16), wq_ref[...], preferred_element_type=F32).astype(BF16)
    for hc in range(2 * PEER_HEADS):
        qs = q[:, hc * KEY_DIM:(hc + 1) * KEY_DIM]
        sc_ref[hc] = lax.dot_general(sk_ref[hc], qs, (((1,), (1,)), ((), ())), preferred_element_type=F32)


def _outproj(x2, o_l, ob, wa, wb, gna, bd, gffn, wq, sk, tm=256):
    t, d = x2.shape
    nseq = (o_l[0].shape[1] * o_l[0].shape[2]) // tm
    row = lambda i: (i, 0)
    fixed = lambda i: (0, 0)
    half = pl.BlockSpec((tm, WIDTH_A), row)
    full = pl.BlockSpec((tm, d), row)
    streams = [pl.BlockSpec((None, a.shape[1], tm // a.shape[1], WIDTH_A),
                            lambda i: (i // nseq, 0, i % nseq, 0)) for a in o_l]
    nq = wq.shape[1]
    return pl.pallas_call(
        _outproj_kernel,
        grid=(t // tm,),
        in_specs=[full] + streams + [half]
                 + [pl.BlockSpec((WIDTH_A, d), fixed), pl.BlockSpec((WIDTH_B, d), fixed),
                    pl.BlockSpec((1, WIDTH_A), fixed), pl.BlockSpec((WIDTH_A, WIDTH_A), fixed),
                    pl.BlockSpec((1, d), fixed), pl.BlockSpec((d, nq), fixed),
                    pl.BlockSpec((2 * PEER_HEADS, N_KEYS, KEY_DIM), lambda i: (0, 0, 0))],
        out_specs=[full, full, pl.BlockSpec((2 * PEER_HEADS, N_KEYS, tm), lambda i: (0, 0, i))],
        out_shape=[jax.ShapeDtypeStruct((t, d), F32), jax.ShapeDtypeStruct((t, d), F32),
                   jax.ShapeDtypeStruct((2 * PEER_HEADS, N_KEYS, t), F32)],
        scratch_shapes=[pltpu.VMEM((WIDTH_A // LANES, tm, LANES), F32)] * len(o_l),
        compiler_params=_cparams(("parallel",)),
        name="outproj",
    )(x2, *o_l, ob, wa, wb, gna, bd, gffn, wq, sk)


def _extract_top(vals, payload, count):
    rows = vals.shape[0]
    iota = lax.broadcasted_iota(jnp.int32, vals.shape, 0).astype(F32)
    tops, picks = [], []
    for _ in range(count):
        m = jnp.max(vals, axis=0, keepdims=True)
        win = jnp.min(jnp.where(vals == m, iota, float(rows)), axis=0, keepdims=True)
        hit = iota == win
        tops.append(m)
        picks.append(win.astype(jnp.int32) if payload is None
                     else jnp.sum(jnp.where(hit, payload, 0), axis=0, keepdims=True))
        vals = jnp.where(hit, NEG_INF, vals)
    return tops, picks


PAIR_COUNTS = tuple(PEER_TOPK // (a + 1) for a in range(PEER_TOPK))
N_PAIRS = sum(PAIR_COUNTS)
PAIR_ROWS = -(-N_PAIRS // 8) * 8


def _topk_kernel(sc_ref, idx_ref, gate_ref, idx_sc, gate_sc, cand_sc, cid_sc):
    tokens = cand_sc.shape[1]
    cand_sc[N_PAIRS:, :] = jnp.full((PAIR_ROWS - N_PAIRS, tokens), NEG_INF, F32)
    cid_sc[N_PAIRS:, :] = jnp.zeros((PAIR_ROWS - N_PAIRS, tokens), jnp.int32)
    for h in range(PEER_HEADS):
        v1, i1 = _extract_top(sc_ref[2 * h], None, PEER_TOPK)
        v2, i2 = _extract_top(sc_ref[2 * h + 1], None, PEER_TOPK)
        v2m = jnp.concatenate(v2, axis=0)
        i2m = jnp.concatenate(i2, axis=0)
        off = 0
        for a, nb in enumerate(PAIR_COUNTS):
            cand_sc[off:off + nb, :] = v1[a] + v2m[0:nb]
            cid_sc[off:off + nb, :] = i1[a] * N_KEYS + i2m[0:nb]
            off += nb
        top, eid = _extract_top(cand_sc[...], cid_sc[...], PEER_TOPK)
        ex = [jnp.exp(tv - top[0]) for tv in top]
        den = ex[0]
        for e in ex[1:]:
            den = den + e
        for k in range(PEER_TOPK):
            j = h * PEER_TOPK + k
            idx_sc[j:j + 1, :] = eid[k]
            gate_sc[j:j + 1, :] = ex[k] / den
    idx_ref[0] = idx_sc[...].T
    gate_ref[0] = gate_sc[...].T


def _topk(scores_t, tt):
    t = scores_t.shape[-1]
    nt = t // tt
    out_spec = pl.BlockSpec((1, tt, PEER_SLOTS), lambda i: (i, 0, 0))
    return pl.pallas_call(
        _topk_kernel,
        grid=(nt,),
        in_specs=[pl.BlockSpec((2 * PEER_HEADS, N_KEYS, tt), lambda i: (0, 0, i))],
        out_specs=[out_spec, out_spec],
        out_shape=[jax.ShapeDtypeStruct((nt, tt, PEER_SLOTS), jnp.int32),
                   jax.ShapeDtypeStruct((nt, tt, PEER_SLOTS), F32)],
        scratch_shapes=[pltpu.VMEM((PEER_SLOTS, tt), jnp.int32), pltpu.VMEM((PEER_SLOTS, tt), F32),
                        pltpu.VMEM((PAIR_ROWS, tt), F32), pltpu.VMEM((PAIR_ROWS, tt), jnp.int32)],
        compiler_params=_cparams(("parallel",)),
        name="topk",
    )(scores_t)


def _gather_rows(slot_ids, tab_ref, stage):
    for j in range(PEER_SLOTS):
        stage[j * ROW_CHUNKS:(j + 1) * ROW_CHUNKS, :] = tab_ref[slot_ids[j]]
    return stage[...]


def _for_each_token(idx_ref, idx_bufs, sems, stages, token_fn, tt):
    group = len(stages)
    nbuf = len(idx_bufs)
    ngroups = tt // group
    ahead = nbuf - 1

    def ids_copy(g, k):
        return pltpu.make_async_copy(idx_ref.at[0, pl.ds(g * group, group)], idx_bufs[k], sems.at[k])

    for g in range(ahead):
        ids_copy(g, g).start()

    def round_(q, carry):
        for k in range(nbuf):
            g = q * nbuf + k
            ids_copy(g, k).wait()

            @pl.when(g + ahead < ngroups)
            def _():
                ids_copy(g + ahead, (k + ahead) % nbuf).start()

            for u, stage in enumerate(stages):
                token_fn(g * group + u, idx_bufs[k].at[u], stage)
        return carry

    lax.fori_loop(0, ngroups // nbuf, round_, 0)


def _gather_scratch():
    return ([pltpu.VMEM((PEER_SLOTS * ROW_CHUNKS, LANES), BF16)] * PEER_GROUP
            + [pltpu.SMEM((PEER_GROUP, PEER_SLOTS), jnp.int32)] * PEER_ID_BUFS
            + [pltpu.SemaphoreType.DMA((PEER_ID_BUFS,))])


def _split_gather_scratch(scratch):
    nid = PEER_GROUP + PEER_ID_BUFS
    return scratch[:PEER_GROUP], scratch[PEER_GROUP:nid], scratch[nid], scratch[nid + 1:]


def _chunk_mask():
    width = PEER_SLOTS * ROW_CHUNKS
    row = lax.broadcasted_iota(jnp.int32, (ROW_CHUNKS, width), 0)
    col = lax.broadcasted_iota(jnp.int32, (ROW_CHUNKS, width), 1)
    return (col & (ROW_CHUNKS - 1)) == row


def _stack_split(a):
    hi, lo = _split_bf16(a)
    return jnp.concatenate([hi, lo], axis=0)


def _to_chunk_major(x_ref, chunk_sc, tt):
    for c in range(ROW_CHUNKS):
        chunk_sc[pl.ds(c, tt, stride=ROW_CHUNKS), :] = x_ref[:, c * LANES:(c + 1) * LANES]


def _peer_dot_kernel(idx_ref, x_ref, tab_ref, gate_ref, fold_ref, act_ref, *scratch, tt):
    stages, idx_bufs, sems, (x_sc, part_sc) = _split_gather_scratch(scratch)
    mask = _chunk_mask()
    _to_chunk_major(x_ref, x_sc, tt)

    def token(t, slot_ids, stage):
        rows = _gather_rows(slot_ids, tab_ref, stage)
        xt = x_sc[pl.ds(pl.multiple_of(t * ROW_CHUNKS, ROW_CHUNKS), ROW_CHUNKS), :]
        lhs = _stack_split(xt)
        half = rows.shape[0] // 2
        wide = jnp.concatenate([rows[:half], rows[half:]], axis=1)
        zero = jnp.zeros_like(lhs)
        lhs2 = jnp.concatenate([jnp.concatenate([lhs, zero], axis=1),
                                jnp.concatenate([zero, lhs], axis=1)], axis=0)
        r2 = lax.dot_general(lhs2, wide, (((1,), (1,)), ((), ())), preferred_element_type=F32)
        r = jnp.concatenate([r2[0:16], r2[16:32]], axis=1)
        r = r[0:ROW_CHUNKS] + r[ROW_CHUNKS:]
        part_sc[t] = jnp.where(mask, r, 0.0)

    _for_each_token(idx_ref, idx_bufs, sems, stages, token, tt)
    part = part_sc[...].reshape(tt * ROW_CHUNKS, PEER_SLOTS * ROW_CHUNKS)
    hi, lo = _split_bf16(part)
    fold = fold_ref[...]
    z = (jnp.dot(hi, fold, preferred_element_type=F32) + jnp.dot(lo, fold, preferred_element_type=F32))
    z = jnp.sum(z.reshape(tt, ROW_CHUNKS, PEER_SLOTS), axis=1)
    gelu = 0.5 * z * (1.0 + lax.erf(z * (1.0 / math.sqrt(2.0))))
    act_ref[0] = gelu * gate_ref[0]


def _peer_dot(idx_t, hn, tab, gate_t, fold, tt):
    nt = idx_t.shape[0]
    d = hn.shape[1]
    tile = lambda i: (i, 0, 0)
    slots = pl.BlockSpec((1, tt, PEER_SLOTS), tile)
    width = PEER_SLOTS * ROW_CHUNKS
    return pl.pallas_call(
        functools.partial(_peer_dot_kernel, tt=tt),
        grid=(nt,),
        in_specs=[slots,
                  pl.BlockSpec((tt, d), lambda i: (i, 0)),
                  pl.BlockSpec(memory_space=pltpu.VMEM),
                  slots,
                  pl.BlockSpec((width, PEER_SLOTS), lambda i: (0, 0))],
        out_specs=slots,
        out_shape=jax.ShapeDtypeStruct((nt, tt, PEER_SLOTS), F32),
        scratch_shapes=_gather_scratch()
                       + [pltpu.VMEM((tt * ROW_CHUNKS, LANES), F32), pltpu.VMEM((tt, ROW_CHUNKS, width), F32)],
        compiler_params=_cparams(("arbitrary",)),
        name="peer_dot",
    )(idx_t, hn, tab, gate_t, fold)


def _peer_out_kernel(idx_ref, act_ref, tab_ref, h_ref, spread_ref, y_ref, *scratch, tt):
    stages, idx_bufs, sems, (y_sc, act_sc) = _split_gather_scratch(scratch)
    mask = _chunk_mask()
    hi, lo = _split_bf16(act_ref[0])
    spread = spread_ref[...]
    act_sc[...] = (jnp.dot(hi, spread, preferred_element_type=F32)
                   + jnp.dot(lo, spread, preferred_element_type=F32))

    def token(t, slot_ids, stage):
        rows = _gather_rows(slot_ids, tab_ref, stage)
        a = jnp.broadcast_to(act_sc[pl.ds(t, 1), :], mask.shape)
        lhs = _stack_split(jnp.where(mask, a, 0.0))
        half = rows.shape[0] // 2
        wide = jnp.concatenate([rows[:half], rows[half:]], axis=1)
        lhs2 = jnp.concatenate([lhs[:, :half], lhs[:, half:]], axis=0)
        o = jnp.dot(lhs2, wide, preferred_element_type=F32)
        o = o[0:16, 0:LANES] + o[16:32, LANES:]
        y_sc[pl.ds(pl.multiple_of(t * ROW_CHUNKS, ROW_CHUNKS), ROW_CHUNKS), :] = o[0:ROW_CHUNKS] + o[ROW_CHUNKS:]

    _for_each_token(idx_ref, idx_bufs, sems, stages, token, tt)
    for c in range(ROW_CHUNKS):
        cols = slice(c * LANES, (c + 1) * LANES)
        y_ref[:, cols] = h_ref[:, cols] + y_sc[pl.ds(c, tt, stride=ROW_CHUNKS), :]


def _peer_out(idx_t, act_t, tab, h, spread, tt):
    nt = idx_t.shape[0]
    tile = lambda i: (i, 0, 0)
    tok = pl.BlockSpec((tt, h.shape[1]), lambda i: (i, 0))
    width = PEER_SLOTS * ROW_CHUNKS
    return pl.pallas_call(
        functools.partial(_peer_out_kernel, tt=tt),
        grid=(nt,),
        in_specs=[pl.BlockSpec((1, tt, PEER_SLOTS), tile),
                  pl.BlockSpec((1, tt, PEER_SLOTS), tile),
                  pl.BlockSpec(memory_space=pltpu.VMEM), tok,
                  pl.BlockSpec((PEER_SLOTS, width), lambda i: (0, 0))],
        out_specs=tok,
        out_shape=jax.ShapeDtypeStruct(h.shape, F32),
        scratch_shapes=_gather_scratch()
                       + [pltpu.VMEM((tt * ROW_CHUNKS, LANES), F32), pltpu.VMEM((tt, width), F32)],
        compiler_params=_cparams(("arbitrary",)),
        name="peer_out",
    )(idx_t, act_t, tab, h, spread)


def _chunk_table(tab):
    n, d = tab.shape
    return tab.astype(BF16).reshape(n, d // LANES, LANES)


def _rope_tables(seq):
    pos = jnp.arange(seq, dtype=F32)
    inv_freq = ROPE_THETA ** (-jnp.arange(0, ROT_DIM, 2, dtype=F32) / ROT_DIM)
    ang = pos[:, None] * inv_freq[None, :]
    cos, sin = jnp.cos(ang), jnp.sin(ang)
    half = ROT_DIM // 2
    pad1 = jnp.ones((seq, HEAD_DIM - ROT_DIM), F32)
    pad0 = jnp.zeros((seq, HEAD_DIM - ROT_DIM), F32)
    zeros = jnp.zeros((seq, half), F32)
    cos_h = jnp.concatenate([cos, cos, pad1], axis=1)
    sa_h = jnp.concatenate([-sin, zeros, pad0], axis=1)
    sb_h = jnp.concatenate([zeros, sin, pad0], axis=1)
    two = lambda a: jnp.concatenate([a, a], axis=1)
    return two(cos_h), two(sa_h), two(sb_h)


def _layer(x2, batch, seq, attn_norm_g, w_in, q_norm_a, k_norm_a, out_norm_a, q_norm_b, k_norm_b,
           lambda_q1, lambda_k1, lambda_q2, lambda_k2, out_norm_b, w_out, ffn_norm_g, w_peer_q,
           peer_sub_keys, peer_u, peer_v, peer_tt=128):
    tile_heads = lambda g: jnp.tile(g.astype(F32), WIDTH_A // HEAD_DIM)[None, :]
    gains = jnp.concatenate([tile_heads(q_norm_a), tile_heads(k_norm_a),
                             tile_heads(q_norm_b), tile_heads(k_norm_b)], axis=0)
    grp = jnp.arange(WIDTH_A) // HEAD_DIM
    bd = (grp[:, None] == grp[None, :]).astype(BF16)
    cos_t, sa_t, sb_t = _rope_tables(seq)

    qa, ka, va, qbt, kb, vbt = _inproj(x2, attn_norm_g[None, :].astype(F32), w_in.astype(BF16), gains, bd,
                                       cos_t, sa_t, sb_t, batch, seq)
    outs = [_dilated(q, k, v) for q, k, v in zip(qa, ka, va)]
    lams = [v[None, :].astype(F32) for v in (lambda_q1, lambda_k1, lambda_q2, lambda_k2)]
    ob = _diffattn(qbt, kb, vbt, lams, out_norm_b[None, :].astype(F32))

    w_out_bf = w_out.astype(BF16)
    sk = peer_sub_keys.reshape(2 * PEER_HEADS, N_KEYS, KEY_DIM).astype(BF16)
    h, hn, scores_t = _outproj(x2, [o for o, _ in outs] + [l for _, l in outs], ob,
                               w_out_bf[:WIDTH_A], w_out_bf[WIDTH_A:], out_norm_a[None, :].astype(F32), bd,
                               ffn_norm_g[None, :].astype(F32), w_peer_q.astype(BF16), sk)
    idx_t, gate_t = _topk(scores_t, peer_tt)
    slot_of = jnp.arange(PEER_SLOTS * ROW_CHUNKS) // ROW_CHUNKS
    spread = (jnp.arange(PEER_SLOTS)[:, None] == slot_of[None, :]).astype(BF16)
    act_t = _peer_dot(idx_t, hn, _chunk_table(peer_u), gate_t, spread.T, peer_tt)
    return _peer_out(idx_t, act_t, _chunk_table(peer_v), h, spread, peer_tt)


def kernel(x, attn_norm_g, w_in, q_norm_a, k_norm_a, out_norm_a, q_norm_b, k_norm_b, lambda_q1, lambda_k1,
           lambda_q2, lambda_k2, out_norm_b, w_out, ffn_norm_g, w_peer_q, peer_sub_keys, peer_u, peer_v):
    batch, seq, d = x.shape
    y = _layer(x.reshape(batch * seq, d), batch, seq, attn_norm_g[0], w_in[0], q_norm_a[0], k_norm_a[0],
               out_norm_a[0], q_norm_b[0], k_norm_b[0], lambda_q1[0], lambda_k1[0], lambda_q2[0],
               lambda_k2[0], out_norm_b[0], w_out[0], ffn_norm_g[0], w_peer_q[0], peer_sub_keys[0],
               peer_u[0], peer_v[0])
    return y.reshape(batch, seq, d)
```

```python
import functools
import math

import jax
import jax.numpy as jnp
from jax import lax
from jax.experimental import pallas as pl
from jax.experimental.pallas import tpu as pltpu

F32 = jnp.float32
BF16 = jnp.bfloat16

HEAD_DIM = 64
N_HEADS_A = 8
WIDTH_A = N_HEADS_A * HEAD_DIM
DILATED_PATTERNS = ((128, 1), (512, 4), (2048, 16))
N_HEADS_B = 4
DIFF_DIM = HEAD_DIM
DIFF_V_DIM = 2 * DIFF_DIM
WIDTH_B = N_HEADS_B * DIFF_V_DIM
ROPE_THETA = 500000.0
ROT_DIM = HEAD_DIM // 4
BLOCK = 128
NORM_EPS = 1e-6
ATTN_SCALE = 1.0 / math.sqrt(HEAD_DIM)
LAM_INIT = 0.8 - 0.6 * math.exp(-0.3 * 0)

PEER_HEADS = 8
N_KEYS = 128
KEY_DIM = 128
PEER_TOPK = 16
PEER_SLOTS = PEER_HEADS * PEER_TOPK
ROW_CHUNKS = 8
PEER_GROUP = 8
PEER_ID_BUFS = 4

LANES = 128
VMEM_LIMIT = 56 * 1024 * 1024

NEG_INF = float("-inf")


def _cparams(sem):
    return pltpu.CompilerParams(dimension_semantics=sem, vmem_limit_bytes=VMEM_LIMIT)


def _split_bf16(a):
    hi = a.astype(BF16)
    lo = (a - hi.astype(F32)).astype(BF16)
    return hi, lo


def _group_sum(sq, bd):
    hi, lo = _split_bf16(sq)
    return (jnp.dot(hi, bd, preferred_element_type=F32)
            + jnp.dot(lo, bd, preferred_element_type=F32))


def _inproj_body(x_ref, g_ref, w_ref, gains_ref, bd_ref, cos_ref, sa_ref, sb_ref,
                 qa_refs, ka_refs, va_refs, qbt_ref, kb_ref, vbt_ref, stream_sc):
    x = x_ref[...]
    ms = jnp.mean(x * x, axis=-1, keepdims=True)
    xn = (x * lax.rsqrt(ms + NORM_EPS) * g_ref[...]).astype(BF16)
    proj = jnp.dot(xn, w_ref[...], preferred_element_type=F32)
    bd = bd_ref[...]
    rep = WIDTH_A // LANES
    cos = jnp.concatenate([cos_ref[...]] * rep, axis=1)
    sa = jnp.concatenate([sa_ref[...]] * rep, axis=1)
    sb = jnp.concatenate([sb_ref[...]] * rep, axis=1)
    half = ROT_DIM // 2

    def norm_rope(z, gain, scale):
        ssum = _group_sum(z * z, bd)
        zn = z * lax.rsqrt(ssum * (1.0 / HEAD_DIM) + NORM_EPS) * gain
        zr = (zn * cos + pltpu.roll(zn, WIDTH_A - half, 1) * sa + pltpu.roll(zn, half, 1) * sb)
        return zr * scale

    w = WIDTH_A
    tm = x.shape[0]
    mixer_a = (norm_rope(proj[:, 0:w], gains_ref[0:1, :], ATTN_SCALE),
               norm_rope(proj[:, w:2 * w], gains_ref[1:2, :], 1.0),
               proj[:, 2 * w:3 * w])
    for z, refs, sc in zip(mixer_a, (qa_refs, ka_refs, va_refs), stream_sc):
        nslab = w // LANES
        for k in range(nslab):
            sc[k] = z[:, k * LANES:(k + 1) * LANES]
        for (_, dil), ref in zip(DILATED_PATTERNS, refs):
            for r in range(dil):
                ref[r] = jnp.concatenate([sc[k, pl.ds(r, tm // dil, stride=dil), :] for k in range(nslab)],
                                         axis=1).astype(BF16)
    qbt_ref[...] = norm_rope(proj[:, 3 * w:4 * w], gains_ref[2:3, :], ATTN_SCALE).T.astype(BF16)
    kb_ref[...] = norm_rope(proj[:, 4 * w:5 * w], gains_ref[3:4, :], 1.0).astype(BF16)
    vbt_ref[...] = proj[:, 5 * w:6 * w].T.astype(BF16)


def _inproj_kernel(x_ref, g_ref, w_ref, gains_ref, bd_ref, cos_ref, sa_ref, sb_ref, *rest):
    nb = len(DILATED_PATTERNS)
    qa_refs, ka_refs, va_refs = rest[0:nb], rest[nb:2 * nb], rest[2 * nb:3 * nb]
    qbt_ref, kb_ref, vbt_ref = rest[3 * nb:3 * nb + 3]
    _inproj_body(x_ref, g_ref, w_ref, gains_ref, bd_ref, cos_ref, sa_ref, sb_ref,
                 qa_refs, ka_refs, va_refs, qbt_ref, kb_ref, vbt_ref, rest[3 * nb + 3:])


def _inproj(x2, g, w_bf, gains, bd, cos_t, sa_t, sb_t, batch, seq, tm=512):
    t, d = x2.shape
    ncol = w_bf.shape[1]
    nseq = seq // tm
    row = lambda i: (i, 0)
    fixed = lambda i: (0, 0)
    pos = lambda i: (i % nseq, 0)
    stream_specs, stream_sds = [], []
    for _, dil in DILATED_PATTERNS:
        stream_specs.append(pl.BlockSpec((None, dil, tm // dil, WIDTH_A), lambda i: (i // nseq, 0, i % nseq, 0)))
        stream_sds.append(jax.ShapeDtypeStruct((batch, dil, seq // dil, WIDTH_A), BF16))
    head_spec = pl.BlockSpec((None, WIDTH_B, tm), lambda i: (i // nseq, 0, i % nseq))
    head_sds = jax.ShapeDtypeStruct((batch, WIDTH_B, seq), BF16)
    outs = pl.pallas_call(
        _inproj_kernel,
        grid=(t // tm,),
        in_specs=[pl.BlockSpec((tm, d), row), pl.BlockSpec((1, d), fixed),
                  pl.BlockSpec((d, ncol), fixed), pl.BlockSpec((4, WIDTH_A), fixed),
                  pl.BlockSpec((WIDTH_A, WIDTH_A), fixed),
                  pl.BlockSpec((tm, LANES), pos), pl.BlockSpec((tm, LANES), pos),
                  pl.BlockSpec((tm, LANES), pos)],
        out_specs=stream_specs * 3 + [head_spec, pl.BlockSpec((tm, WIDTH_B), row), head_spec],
        out_shape=stream_sds * 3 + [head_sds, jax.ShapeDtypeStruct((t, WIDTH_B), BF16), head_sds],
        scratch_shapes=[pltpu.VMEM((WIDTH_A // LANES, tm, LANES), F32)] * 3,
        compiler_params=_cparams(("parallel",)),
        name="inproj",
    )(x2, g, w_bf, gains, bd, cos_t, sa_t, sb_t)
    nb = len(DILATED_PATTERNS)
    return outs[0:nb], outs[nb:2 * nb], outs[2 * nb:3 * nb], outs[3 * nb], outs[3 * nb + 1], outs[3 * nb + 2]


def _dilated_kernel(q_ref, kp_ref, k_ref, vp_ref, v_ref, o_ref, l_ref, kbuf, vbuf, *, tq):
    tile = pl.program_id(2)
    kbuf[0:BLOCK, :] = kp_ref[...]
    kbuf[BLOCK:, :] = k_ref[...]
    vbuf[0:BLOCK, :] = vp_ref[...]
    vbuf[BLOCK:, :] = v_ref[...]

    lane = lax.broadcasted_iota(jnp.int32, (BLOCK, LANES), 1)
    even = lane < HEAD_DIM
    head_masks = (even, lane >= HEAD_DIM)
    qi = lax.broadcasted_iota(jnp.int32, (BLOCK, 2 * BLOCK), 0)
    kj = lax.broadcasted_iota(jnp.int32, (BLOCK, 2 * BLOCK), 1)
    dist = qi + BLOCK - kj
    band = (dist >= 0) & (dist <= BLOCK)

    def body(blk, carry):
        r0 = pl.multiple_of(blk * BLOCK, BLOCK)
        first_key = jnp.where((tile == 0) & (blk == 0), BLOCK, 0)
        valid = band & (kj >= first_key)
        for pair in range(WIDTH_A // LANES):
            cols = slice(pair * LANES, (pair + 1) * LANES)
            q = q_ref[pl.ds(r0, BLOCK), cols]
            kk = kbuf[pl.ds(r0, 2 * BLOCK), cols]
            vv = vbuf[pl.ds(r0, 2 * BLOCK), cols]
            outs, lses = [], []
            for head_mask in head_masks:
                qh = jnp.where(head_mask, q, jnp.zeros_like(q))
                s = lax.dot_general(qh, kk, (((1,), (1,)), ((), ())), preferred_element_type=F32)
                s = jnp.where(valid, s, NEG_INF)
                m = jnp.max(s, axis=-1, keepdims=True)
                p = jnp.exp(s - m)
                den = jnp.sum(p, axis=-1, keepdims=True)
                o = jnp.dot(p.astype(BF16), vv, preferred_element_type=F32) / den
                outs.append(o)
                lses.append(jnp.broadcast_to(m + jnp.log(den), (BLOCK, LANES)))
            o_ref[pl.ds(r0, BLOCK), cols] = jnp.where(even, outs[0], outs[1])
            l_ref[pl.ds(r0, BLOCK), cols] = jnp.where(even, lses[0], lses[1])
        return carry

    lax.fori_loop(0, tq // BLOCK, body, 0)


def _dilated(qa, ka, va):
    batch, dilation, stream, _ = qa.shape
    tq = min(stream, 1024)
    nprev = tq // BLOCK
    cur = lambda b, r, i: (b, r, i, 0)
    prev = lambda b, r, i: (b, r, jnp.maximum(i * nprev - 1, 0), 0)
    blk_cur = pl.BlockSpec((None, None, tq, WIDTH_A), cur)
    blk_prev = pl.BlockSpec((None, None, BLOCK, WIDTH_A), prev)
    out_sds = jax.ShapeDtypeStruct(qa.shape, F32)
    return pl.pallas_call(
        functools.partial(_dilated_kernel, tq=tq),
        grid=(batch, dilation, stream // tq),
        in_specs=[blk_cur, blk_prev, blk_cur, blk_prev, blk_cur],
        out_specs=[blk_cur, blk_cur],
        out_shape=[out_sds, out_sds],
        scratch_shapes=[pltpu.VMEM((tq + BLOCK, WIDTH_A), BF16), pltpu.VMEM((tq + BLOCK, WIDTH_A), BF16)],
        compiler_params=_cparams(("parallel", "parallel", "arbitrary")),
        name=f"dilated_d{dilation}",
    )(qa, ka, ka, va, va)


def _diff_kernel(lq1_ref, lk1_ref, lq2_ref, lk2_ref, gain_ref, qt_ref, k_ref, vt_ref, o_ref,
                 m_sc, l_sc, acc_sc, *, tq):
    i = pl.program_id(2)
    heads = qt_ref.shape[0] // DIFF_V_DIM
    dim = lax.broadcasted_iota(jnp.int32, (DIFF_V_DIM, tq), 0)
    qs = []
    for hh in range(heads):
        qt = qt_ref[hh * DIFF_V_DIM:(hh + 1) * DIFF_V_DIM, :]
        zero = jnp.zeros_like(qt)
        qs += [jnp.where(dim < DIFF_DIM, qt, zero), jnp.where(dim >= DIFF_DIM, qt, zero)]

    m_sc[...] = jnp.full(m_sc.shape, NEG_INF, F32)
    l_sc[...] = jnp.zeros(l_sc.shape, F32)
    acc_sc[...] = jnp.zeros(acc_sc.shape, F32)

    key = lax.broadcasted_iota(jnp.int32, (tq, tq), 0)
    query = lax.broadcasted_iota(jnp.int32, (tq, tq), 1)
    causal = key <= query

    def step(j, masked):
        r0 = pl.multiple_of(j * tq, tq)
        for hh in range(heads):
            cols = slice(hh * DIFF_V_DIM, (hh + 1) * DIFF_V_DIM)
            kb = k_ref[pl.ds(r0, tq), cols]
            vtb = vt_ref[cols, pl.ds(r0, tq)]
            for c in range(2 * hh, 2 * hh + 2):
                s = jnp.dot(kb, qs[c], preferred_element_type=F32)
                if masked:
                    s = jnp.where(causal, s, NEG_INF)
                m_old = m_sc[c]
                m_new = jnp.maximum(m_old, jnp.max(s, axis=0, keepdims=True))
                alpha = jnp.exp(m_old - m_new)
                p = jnp.exp(s - m_new)
                l_sc[c] = alpha * l_sc[c] + jnp.sum(p, axis=0, keepdims=True)
                acc_sc[c] = alpha * acc_sc[c] + jnp.dot(vtb, p.astype(BF16), preferred_element_type=F32)
                m_sc[c] = m_new

    def body(j, carry):
        step(j, False)
        return carry

    lax.fori_loop(0, i, body, 0)
    step(i, True)

    lam = (jnp.exp(jnp.sum(lq1_ref[...] * lk1_ref[...], keepdims=True))
           - jnp.exp(jnp.sum(lq2_ref[...] * lk2_ref[...], keepdims=True)) + LAM_INIT)
    for hh in range(heads):
        cols = slice(hh * DIFF_V_DIM, (hh + 1) * DIFF_V_DIM)
        ot = acc_sc[2 * hh] / l_sc[2 * hh] - lam * (acc_sc[2 * hh + 1] / l_sc[2 * hh + 1])
        o = ot.T
        ms = jnp.mean(o * o, axis=-1, keepdims=True)
        o = o * lax.rsqrt(ms + NORM_EPS) * gain_ref[:, cols] * (1.0 - LAM_INIT)
        o_ref[:, cols] = o.astype(o_ref.dtype)


def _diffattn(qbt, kb, vbt, lams, gain_b, tq=512, heads=2):
    batch, _, seq = qbt.shape
    t = batch * seq
    nq = seq // tq
    width = heads * DIFF_V_DIM
    lam_spec = pl.BlockSpec((1, DIFF_DIM), lambda b, h, i: (0, 0))
    k_spec = pl.BlockSpec((seq, width), lambda b, h, i: (b, h))
    vt_spec = pl.BlockSpec((None, width, seq), lambda b, h, i: (b, h, 0))
    qt_spec = pl.BlockSpec((None, width, tq), lambda b, h, i: (b, h, i))
    o_spec = pl.BlockSpec((tq, width), lambda b, h, i: (b * nq + i, h))
    return pl.pallas_call(
        functools.partial(_diff_kernel, tq=tq),
        grid=(batch, N_HEADS_B // heads, nq),
        in_specs=[lam_spec] * 4 + [pl.BlockSpec((1, width), lambda b, h, i: (0, h)), qt_spec, k_spec, vt_spec],
        out_specs=o_spec,
        out_shape=jax.ShapeDtypeStruct((t, WIDTH_B), BF16),
        scratch_shapes=[pltpu.VMEM((2 * heads, 1, tq), F32), pltpu.VMEM((2 * heads, 1, tq), F32),
                        pltpu.VMEM((2 * heads, DIFF_V_DIM, tq), F32)],
        compiler_params=_cparams(("parallel", "parallel", "arbitrary")),
        name="diffattn",
    )(*lams, gain_b, qbt, kb, vbt)


def _outproj_kernel(x_ref, o1_ref, o2_ref, o3_ref, l1_ref, l2_ref, l3_ref, ob_ref, wa_ref, wb_ref,
                    gna_ref, bd_ref, gffn_ref, wq_ref, sk_ref, h_ref, hn_ref, sc_ref, *order_sc):
    tm = x_ref.shape[0]

    def token_order(ref, sc):
        dil = ref.shape[0]
        if dil == 1:
            return ref[0]
        nslab = WIDTH_A // LANES
        for r in range(dil):
            for k in range(nslab):
                sc[k, pl.ds(r, tm // dil, stride=dil), :] = ref[r, :, k * LANES:(k + 1) * LANES]
        return jnp.concatenate([sc[k] for k in range(nslab)], axis=1)

    o1, o2, o3, l1, l2, l3 = [token_order(ref, sc) for ref, sc in
                              zip((o1_ref, o2_ref, o3_ref, l1_ref, l2_ref, l3_ref), order_sc)]
    lmax = jnp.maximum(jnp.maximum(l1, l2), l3)
    e1, e2, e3 = jnp.exp(l1 - lmax), jnp.exp(l2 - lmax), jnp.exp(l3 - lmax)
    oa = (e1 * o1 + e2 * o2 + e3 * o3) / (e1 + e2 + e3)
    ssum = _group_sum(oa * oa, bd_ref[...])
    oa = oa * lax.rsqrt(ssum * (1.0 / HEAD_DIM) + NORM_EPS) * gna_ref[...]
    h = (x_ref[...]
         + jnp.dot(oa.astype(BF16), wa_ref[...], preferred_element_type=F32)
         + jnp.dot(ob_ref[...], wb_ref[...], preferred_element_type=F32))
    h_ref[...] = h
    ms = jnp.mean(h * h, axis=-1, keepdims=True)
    hn = h * lax.rsqrt(ms + NORM_EPS) * gffn_ref[...]
    hn_ref[...] = hn
    q = jnp.dot(hn.astype(BF16), wq_ref[...], preferred_element_type=F32).astype(BF16)
    for hc in range(2 * PEER_HEADS):
        qs = q[:, hc * KEY_DIM:(hc + 1) * KEY_DIM]
        sc_ref[hc] = lax.dot_general(sk_ref[hc], qs, (((1,), (1,)), ((), ())), preferred_element_type=F32)


def _outproj(x2, o_l, ob, wa, wb, gna, bd, gffn, wq, sk, tm=256):
    t, d = x2.shape
    nseq = (o_l[0].shape[1] * o_l[0].shape[2]) // tm
    row = lambda i: (i, 0)
    fixed = lambda i: (0, 0)
    half = pl.BlockSpec((tm, WIDTH_A), row)
    full = pl.BlockSpec((tm, d), row)
    streams = [pl.BlockSpec((None, a.shape[1], tm // a.shape[1], WIDTH_A),
                            lambda i: (i // nseq, 0, i % nseq, 0)) for a in o_l]
    nq = wq.shape[1]
    return pl.pallas_call(
        _outproj_kernel,
        grid=(t // tm,),
        in_specs=[full] + streams + [half]
                 + [pl.BlockSpec((WIDTH_A, d), fixed), pl.BlockSpec((WIDTH_B, d), fixed),
                    pl.BlockSpec((1, WIDTH_A), fixed), pl.BlockSpec((WIDTH_A, WIDTH_A), fixed),
                    pl.BlockSpec((1, d), fixed), pl.BlockSpec((d, nq), fixed),
                    pl.BlockSpec((2 * PEER_HEADS, N_KEYS, KEY_DIM), lambda i: (0, 0, 0))],
        out_specs=[full, full, pl.BlockSpec((2 * PEER_HEADS, N_KEYS, tm), lambda i: (0, 0, i))],
        out_shape=[jax.ShapeDtypeStruct((t, d), F32), jax.ShapeDtypeStruct((t, d), F32),
                   jax.ShapeDtypeStruct((2 * PEER_HEADS, N_KEYS, t), F32)],
        scratch_shapes=[pltpu.VMEM((WIDTH_A // LANES, tm, LANES), F32)] * len(o_l),
        compiler_params=_cparams(("parallel",)),
        name="outproj",
    )(x2, *o_l, ob, wa, wb, gna, bd, gffn, wq, sk)


def _extract_top(vals, payload, count):
    rows = vals.shape[0]
    iota = lax.broadcasted_iota(jnp.int32, vals.shape, 0).astype(F32)
    tops, picks = [], []
    for _ in range(count):
        m = jnp.max(vals, axis=0, keepdims=True)
        win = jnp.min(jnp.where(vals == m, iota, float(rows)), axis=0, keepdims=True)
        hit = iota == win
        tops.append(m)
        picks.append(win.astype(jnp.int32) if payload is None
                     else jnp.sum(jnp.where(hit, payload, 0), axis=0, keepdims=True))
        vals = jnp.where(hit, NEG_INF, vals)
    return tops, picks


PAIR_COUNTS = tuple(PEER_TOPK // (a + 1) for a in range(PEER_TOPK))
N_PAIRS = sum(PAIR_COUNTS)
PAIR_ROWS = -(-N_PAIRS // 8) * 8


def _topk_kernel(sc_ref, idx_ref, gate_ref, idx_sc, gate_sc, cand_sc, cid_sc):
    tokens = cand_sc.shape[1]
    cand_sc[N_PAIRS:, :] = jnp.full((PAIR_ROWS - N_PAIRS, tokens), NEG_INF, F32)
    cid_sc[N_PAIRS:, :] = jnp.zeros((PAIR_ROWS - N_PAIRS, tokens), jnp.int32)
    for h in range(PEER_HEADS):
        v1, i1 = _extract_top(sc_ref[2 * h], None, PEER_TOPK)
        v2, i2 = _extract_top(sc_ref[2 * h + 1], None, PEER_TOPK)
        v2m = jnp.concatenate(v2, axis=0)
        i2m = jnp.concatenate(i2, axis=0)
        off = 0
        for a, nb in enumerate(PAIR_COUNTS):
            cand_sc[off:off + nb, :] = v1[a] + v2m[0:nb]
            cid_sc[off:off + nb, :] = i1[a] * N_KEYS + i2m[0:nb]
            off += nb
        top, eid = _extract_top(cand_sc[...], cid_sc[...], PEER_TOPK)
        ex = [jnp.exp(tv - top[0]) for tv in top]
        den = ex[0]
        for e in ex[1:]:
            den = den + e
        for k in range(PEER_TOPK):
            j = h * PEER_TOPK + k
            idx_sc[j:j + 1, :] = eid[k]
            gate_sc[j:j + 1, :] = ex[k] / den
    idx_ref[0] = idx_sc[...].T
    gate_ref[0] = gate_sc[...].T


def _topk(scores_t, tt):
    t = scores_t.shape[-1]
    nt = t // tt
    out_spec = pl.BlockSpec((1, tt, PEER_SLOTS), lambda i: (i, 0, 0))
    return pl.pallas_call(
        _topk_kernel,
        grid=(nt,),
        in_specs=[pl.BlockSpec((2 * PEER_HEADS, N_KEYS, tt), lambda i: (0, 0, i))],
        out_specs=[out_spec, out_spec],
        out_shape=[jax.ShapeDtypeStruct((nt, tt, PEER_SLOTS), jnp.int32),
                   jax.ShapeDtypeStruct((nt, tt, PEER_SLOTS), F32)],
        scratch_shapes=[pltpu.VMEM((PEER_SLOTS, tt), jnp.int32), pltpu.VMEM((PEER_SLOTS, tt), F32),
                        pltpu.VMEM((PAIR_ROWS, tt), F32), pltpu.VMEM((PAIR_ROWS, tt), jnp.int32)],
        compiler_params=_cparams(("parallel",)),
        name="topk",
    )(scores_t)


def _gather_rows(slot_ids, tab_ref, stage):
    for j in range(PEER_SLOTS):
        stage[j * ROW_CHUNKS:(j + 1) * ROW_CHUNKS, :] = tab_ref[slot_ids[j]]
    return stage[...]


def _for_each_token(idx_ref, idx_bufs, sems, stages, token_fn, tt):
    group = len(stages)
    nbuf = len(idx_bufs)
    ngroups = tt // group
    ahead = nbuf - 1

    def ids_copy(g, k):
        return pltpu.make_async_copy(idx_ref.at[0, pl.ds(g * group, group)], idx_bufs[k], sems.at[k])

    for g in range(ahead):
        ids_copy(g, g).start()

    def round_(q, carry):
        for k in range(nbuf):
            g = q * nbuf + k
            ids_copy(g, k).wait()

            @pl.when(g + ahead < ngroups)
            def _():
                ids_copy(g + ahead, (k + ahead) % nbuf).start()

            for u, stage in enumerate(stages):
                token_fn(g * group + u, idx_bufs[k].at[u], stage)
        return carry

    lax.fori_loop(0, ngroups // nbuf, round_, 0)


def _id_scratch():
    return [pltpu.SMEM((PEER_GROUP, PEER_SLOTS), jnp.int32)] * PEER_ID_BUFS + [pltpu.SemaphoreType.DMA((PEER_ID_BUFS,))]


def _gather_scratch():
    return [pltpu.VMEM((PEER_SLOTS * ROW_CHUNKS, LANES), BF16)] * PEER_GROUP + _id_scratch()


def _split_gather_scratch(scratch):
    nid = PEER_GROUP + PEER_ID_BUFS
    return scratch[:PEER_GROUP], scratch[PEER_GROUP:nid], scratch[nid], scratch[nid + 1:]


def _chunk_mask():
    width = PEER_SLOTS * ROW_CHUNKS
    row = lax.broadcasted_iota(jnp.int32, (ROW_CHUNKS, width), 0)
    col = lax.broadcasted_iota(jnp.int32, (ROW_CHUNKS, width), 1)
    return (col & (ROW_CHUNKS - 1)) == row


def _stack_split(a):
    hi, lo = _split_bf16(a)
    return jnp.concatenate([hi, lo], axis=0)


def _to_chunk_major(x_ref, chunk_sc, tt):
    for c in range(ROW_CHUNKS):
        chunk_sc[pl.ds(c, tt, stride=ROW_CHUNKS), :] = x_ref[:, c * LANES:(c + 1) * LANES]


def _peer_dot_kernel(idx_ref, x_ref, tab_ref, gate_ref, fold_ref, act_ref, *scratch, tt):
    stages, idx_bufs, sems, (x_sc, part_sc) = _split_gather_scratch(scratch)
    mask = _chunk_mask()
    _to_chunk_major(x_ref, x_sc, tt)

    def token(t, slot_ids, stage):
        rows = _gather_rows(slot_ids, tab_ref, stage)
        xt = x_sc[pl.ds(pl.multiple_of(t * ROW_CHUNKS, ROW_CHUNKS), ROW_CHUNKS), :]
        lhs = _stack_split(xt)
        r = lax.dot_general(lhs, rows, (((1,), (1,)), ((), ())), preferred_element_type=F32)
        r = r[0:ROW_CHUNKS] + r[ROW_CHUNKS:]
        part_sc[t] = jnp.where(mask, r, 0.0)

    _for_each_token(idx_ref, idx_bufs, sems, stages, token, tt)
    part = part_sc[...].reshape(tt * ROW_CHUNKS, PEER_SLOTS * ROW_CHUNKS)
    hi, lo = _split_bf16(part)
    fold = fold_ref[...]
    z = (jnp.dot(hi, fold, preferred_element_type=F32) + jnp.dot(lo, fold, preferred_element_type=F32))
    z = jnp.sum(z.reshape(tt, ROW_CHUNKS, PEER_SLOTS), axis=1)
    gelu = 0.5 * z * (1.0 + lax.erf(z * (1.0 / math.sqrt(2.0))))
    act_ref[0] = gelu * gate_ref[0]


def _peer_dot(idx_t, hn, tab, gate_t, fold, tt):
    nt = idx_t.shape[0]
    d = hn.shape[1]
    tile = lambda i: (i, 0, 0)
    slots = pl.BlockSpec((1, tt, PEER_SLOTS), tile)
    width = PEER_SLOTS * ROW_CHUNKS
    return pl.pallas_call(
        functools.partial(_peer_dot_kernel, tt=tt),
        grid=(nt,),
        in_specs=[slots,
                  pl.BlockSpec((tt, d), lambda i: (i, 0)),
                  pl.BlockSpec(memory_space=pltpu.VMEM),
                  slots,
                  pl.BlockSpec((width, PEER_SLOTS), lambda i: (0, 0))],
        out_specs=slots,
        out_shape=jax.ShapeDtypeStruct((nt, tt, PEER_SLOTS), F32),
        scratch_shapes=_gather_scratch()
                       + [pltpu.VMEM((tt * ROW_CHUNKS, LANES), F32), pltpu.VMEM((tt, ROW_CHUNKS, width), F32)],
        compiler_params=_cparams(("arbitrary",)),
        name="peer_dot",
    )(idx_t, hn, tab, gate_t, fold)


def _peer_out_kernel(idx_ref, act_ref, tab_ref, h_ref, y_ref, *scratch, tt):
    idx_bufs, sems, y_sc, w_sc = scratch[:PEER_ID_BUFS], scratch[PEER_ID_BUFS], scratch[-2], scratch[-1]
    act_t = act_ref[0].T
    for t in range(tt):
        w_sc[t * PEER_SLOTS:(t + 1) * PEER_SLOTS, :] = jnp.broadcast_to(act_t[:, t:t + 1], (PEER_SLOTS, LANES))
    nacc = 4

    def token(t, slot_ids, _):
        weights = w_sc.at[pl.ds(pl.multiple_of(t * PEER_SLOTS, PEER_SLOTS), PEER_SLOTS)]
        acc = [None] * nacc
        for j in range(PEER_SLOTS):
            row = tab_ref[slot_ids[j]].astype(F32)
            term = jnp.broadcast_to(weights[j:j + 1, :], row.shape) * row
            acc[j % nacc] = term if acc[j % nacc] is None else acc[j % nacc] + term
        y_sc[pl.ds(pl.multiple_of(t * ROW_CHUNKS, ROW_CHUNKS), ROW_CHUNKS), :] = (acc[0] + acc[1]) + (acc[2] + acc[3])

    _for_each_token(idx_ref, idx_bufs, sems, (None,) * PEER_GROUP, token, tt)
    for c in range(ROW_CHUNKS):
        cols = slice(c * LANES, (c + 1) * LANES)
        y_ref[:, cols] = h_ref[:, cols] + y_sc[pl.ds(c, tt, stride=ROW_CHUNKS), :]


def _peer_out(idx_t, act_t, tab, h, tt):
    nt = idx_t.shape[0]
    tile = lambda i: (i, 0, 0)
    tok = pl.BlockSpec((tt, h.shape[1]), lambda i: (i, 0))
    return pl.pallas_call(
        functools.partial(_peer_out_kernel, tt=tt),
        grid=(nt,),
        in_specs=[pl.BlockSpec((1, tt, PEER_SLOTS), tile),
                  pl.BlockSpec((1, tt, PEER_SLOTS), tile),
                  pl.BlockSpec(memory_space=pltpu.VMEM), tok],
        out_specs=tok,
        out_shape=jax.ShapeDtypeStruct(h.shape, F32),
        scratch_shapes=_id_scratch()
                       + [pltpu.VMEM((tt * ROW_CHUNKS, LANES), F32), pltpu.VMEM((tt * PEER_SLOTS, LANES), F32)],
        compiler_params=_cparams(("arbitrary",)),
        name="peer_out",
    )(idx_t, act_t, tab, h)


def _chunk_table(tab):
    n, d = tab.shape
    return tab.astype(BF16).reshape(n, d // LANES, LANES)


def _rope_tables(seq):
    pos = jnp.arange(seq, dtype=F32)
    inv_freq = ROPE_THETA ** (-jnp.arange(0, ROT_DIM, 2, dtype=F32) / ROT_DIM)
    ang = pos[:, None] * inv_freq[None, :]
    cos, sin = jnp.cos(ang), jnp.sin(ang)
    half = ROT_DIM // 2
    pad1 = jnp.ones((seq, HEAD_DIM - ROT_DIM), F32)
    pad0 = jnp.zeros((seq, HEAD_DIM - ROT_DIM), F32)
    zeros = jnp.zeros((seq, half), F32)
    cos_h = jnp.concatenate([cos, cos, pad1], axis=1)
    sa_h = jnp.concatenate([-sin, zeros, pad0], axis=1)
    sb_h = jnp.concatenate([zeros, sin, pad0], axis=1)
    two = lambda a: jnp.concatenate([a, a], axis=1)
    return two(cos_h), two(sa_h), two(sb_h)


def _layer(x2, batch, seq, attn_norm_g, w_in, q_norm_a, k_norm_a, out_norm_a, q_norm_b, k_norm_b,
           lambda_q1, lambda_k1, lambda_q2, lambda_k2, out_norm_b, w_out, ffn_norm_g, w_peer_q,
           peer_sub_keys, peer_u, peer_v, peer_tt=128):
    tile_heads = lambda g: jnp.tile(g.astype(F32), WIDTH_A // HEAD_DIM)[None, :]
    gains = jnp.concatenate([tile_heads(q_norm_a), tile_heads(k_norm_a),
                             tile_heads(q_norm_b), tile_heads(k_norm_b)], axis=0)
    grp = jnp.arange(WIDTH_A) // HEAD_DIM
    bd = (grp[:, None] == grp[None, :]).astype(BF16)
    cos_t, sa_t, sb_t = _rope_tables(seq)

    qa, ka, va, qbt, kb, vbt = _inproj(x2, attn_norm_g[None, :].astype(F32), w_in.astype(BF16), gains, bd,
                                       cos_t, sa_t, sb_t, batch, seq)
    outs = [_dilated(q, k, v) for q, k, v in zip(qa, ka, va)]
    lams = [v[None, :].astype(F32) for v in (lambda_q1, lambda_k1, lambda_q2, lambda_k2)]
    ob = _diffattn(qbt, kb, vbt, lams, out_norm_b[None, :].astype(F32))

    w_out_bf = w_out.astype(BF16)
    sk = peer_sub_keys.reshape(2 * PEER_HEADS, N_KEYS, KEY_DIM).astype(BF16)
    h, hn, scores_t = _outproj(x2, [o for o, _ in outs] + [l for _, l in outs], ob,
                               w_out_bf[:WIDTH_A], w_out_bf[WIDTH_A:], out_norm_a[None, :].astype(F32), bd,
                               ffn_norm_g[None, :].astype(F32), w_peer_q.astype(BF16), sk)
    idx_t, gate_t = _topk(scores_t, peer_tt)
    slot_of = jnp.arange(PEER_SLOTS * ROW_CHUNKS) // ROW_CHUNKS
    spread = (jnp.arange(PEER_SLOTS)[:, None] == slot_of[None, :]).astype(BF16)
    act_t = _peer_dot(idx_t, hn, _chunk_table(peer_u), gate_t, spread.T, peer_tt)
    return _peer_out(idx_t, act_t, _chunk_table(peer_v), h, peer_tt)


def kernel(x, attn_norm_g, w_in, q_norm_a, k_norm_a, out_norm_a, q_norm_b, k_norm_b, lambda_q1, lambda_k1,
           lambda_q2, lambda_k2, out_norm_b, w_out, ffn_norm_g, w_peer_q, peer_sub_keys, peer_u, peer_v):
    batch, seq, d = x.shape
    y = _layer(x.reshape(batch * seq, d), batch, seq, attn_norm_g[0], w_in[0], q_norm_a[0], k_norm_a[0],
               out_norm_a[0], q_norm_b[0], k_norm_b[0], lambda_q1[0], lambda_k1[0], lambda_q2[0],
               lambda_k2[0], out_norm_b[0], w_out[0], ffn_norm_g[0], w_peer_q[0], peer_sub_keys[0],
               peer_u[0], peer_v[0])
    return y.reshape(batch, seq, d)
```

```python
import functools
import math

import jax
import jax.numpy as jnp
from jax import lax
from jax.experimental import pallas as pl
from jax.experimental.pallas import tpu as pltpu

F32 = jnp.float32
BF16 = jnp.bfloat16

HEAD_DIM = 64
N_HEADS_A = 8
WIDTH_A = N_HEADS_A * HEAD_DIM
DILATED_PATTERNS = ((128, 1), (512, 4), (2048, 16))
N_HEADS_B = 4
DIFF_DIM = HEAD_DIM
DIFF_V_DIM = 2 * DIFF_DIM
WIDTH_B = N_HEADS_B * DIFF_V_DIM
ROPE_THETA = 500000.0
ROT_DIM = HEAD_DIM // 4
BLOCK = 128
NORM_EPS = 1e-6
ATTN_SCALE = 1.0 / math.sqrt(HEAD_DIM)
LAM_INIT = 0.8 - 0.6 * math.exp(-0.3 * 0)

PEER_HEADS = 8
N_KEYS = 128
KEY_DIM = 128
PEER_TOPK = 16
PEER_SLOTS = PEER_HEADS * PEER_TOPK
ROW_CHUNKS = 8
PEER_GROUP = 8
PEER_ID_BUFS = 4

LANES = 128
VMEM_LIMIT = 56 * 1024 * 1024

NEG_INF = float("-inf")


def _cparams(sem):
    return pltpu.CompilerParams(dimension_semantics=sem, vmem_limit_bytes=VMEM_LIMIT)


def _split_bf16(a):
    hi = a.astype(BF16)
    lo = (a - hi.astype(F32)).astype(BF16)
    return hi, lo


def _group_sum(sq, bd):
    hi, lo = _split_bf16(sq)
    return (jnp.dot(hi, bd, preferred_element_type=F32)
            + jnp.dot(lo, bd, preferred_element_type=F32))


def _inproj_body(x_ref, g_ref, w_ref, gains_ref, bd_ref, cos_ref, sa_ref, sb_ref,
                 qa_refs, ka_refs, va_refs, qbt_ref, kb_ref, vbt_ref, stream_sc):
    x = x_ref[...]
    ms = jnp.mean(x * x, axis=-1, keepdims=True)
    xn = (x * lax.rsqrt(ms + NORM_EPS) * g_ref[...]).astype(BF16)
    proj = jnp.dot(xn, w_ref[...], preferred_element_type=F32)
    bd = bd_ref[...]
    rep = WIDTH_A // LANES
    cos = jnp.concatenate([cos_ref[...]] * rep, axis=1)
    sa = jnp.concatenate([sa_ref[...]] * rep, axis=1)
    sb = jnp.concatenate([sb_ref[...]] * rep, axis=1)
    half = ROT_DIM // 2

    def norm_rope(z, gain, scale):
        ssum = _group_sum(z * z, bd)
        zn = z * lax.rsqrt(ssum * (1.0 / HEAD_DIM) + NORM_EPS) * gain
        zr = (zn * cos + pltpu.roll(zn, WIDTH_A - half, 1) * sa + pltpu.roll(zn, half, 1) * sb)
        return zr * scale

    w = WIDTH_A
    tm = x.shape[0]
    mixer_a = (norm_rope(proj[:, 0:w], gains_ref[0:1, :], ATTN_SCALE),
               norm_rope(proj[:, w:2 * w], gains_ref[1:2, :], 1.0),
               proj[:, 2 * w:3 * w])
    for z, refs, sc in zip(mixer_a, (qa_refs, ka_refs, va_refs), stream_sc):
        nslab = w // LANES
        for k in range(nslab):
            sc[k] = z[:, k * LANES:(k + 1) * LANES]
        for (_, dil), ref in zip(DILATED_PATTERNS, refs):
            for r in range(dil):
                ref[r] = jnp.concatenate([sc[k, pl.ds(r, tm // dil, stride=dil), :] for k in range(nslab)],
                                         axis=1).astype(BF16)
    qbt_ref[...] = norm_rope(proj[:, 3 * w:4 * w], gains_ref[2:3, :], ATTN_SCALE).T.astype(BF16)
    kb_ref[...] = norm_rope(proj[:, 4 * w:5 * w], gains_ref[3:4, :], 1.0).astype(BF16)
    vbt_ref[...] = proj[:, 5 * w:6 * w].T.astype(BF16)


def _inproj_kernel(x_ref, g_ref, w_ref, gains_ref, bd_ref, cos_ref, sa_ref, sb_ref, *rest):
    nb = len(DILATED_PATTERNS)
    qa_refs, ka_refs, va_refs = rest[0:nb], rest[nb:2 * nb], rest[2 * nb:3 * nb]
    qbt_ref, kb_ref, vbt_ref = rest[3 * nb:3 * nb + 3]
    _inproj_body(x_ref, g_ref, w_ref, gains_ref, bd_ref, cos_ref, sa_ref, sb_ref,
                 qa_refs, ka_refs, va_refs, qbt_ref, kb_ref, vbt_ref, rest[3 * nb + 3:])


def _inproj(x2, g, w_bf, gains, bd, cos_t, sa_t, sb_t, batch, seq, tm=512):
    t, d = x2.shape
    ncol = w_bf.shape[1]
    nseq = seq // tm
    row = lambda i: (i, 0)
    fixed = lambda i: (0, 0)
    pos = lambda i: (i % nseq, 0)
    stream_specs, stream_sds = [], []
    for _, dil in DILATED_PATTERNS:
        stream_specs.append(pl.BlockSpec((None, dil, tm // dil, WIDTH_A), lambda i: (i // nseq, 0, i % nseq, 0)))
        stream_sds.append(jax.ShapeDtypeStruct((batch, dil, seq // dil, WIDTH_A), BF16))
    head_spec = pl.BlockSpec((None, WIDTH_B, tm), lambda i: (i // nseq, 0, i % nseq))
    head_sds = jax.ShapeDtypeStruct((batch, WIDTH_B, seq), BF16)
    outs = pl.pallas_call(
        _inproj_kernel,
        grid=(t // tm,),
        in_specs=[pl.BlockSpec((tm, d), row), pl.BlockSpec((1, d), fixed),
                  pl.BlockSpec((d, ncol), fixed), pl.BlockSpec((4, WIDTH_A), fixed),
                  pl.BlockSpec((WIDTH_A, WIDTH_A), fixed),
                  pl.BlockSpec((tm, LANES), pos), pl.BlockSpec((tm, LANES), pos),
                  pl.BlockSpec((tm, LANES), pos)],
        out_specs=stream_specs * 3 + [head_spec, pl.BlockSpec((tm, WIDTH_B), row), head_spec],
        out_shape=stream_sds * 3 + [head_sds, jax.ShapeDtypeStruct((t, WIDTH_B), BF16), head_sds],
        scratch_shapes=[pltpu.VMEM((WIDTH_A // LANES, tm, LANES), F32)] * 3,
        compiler_params=_cparams(("parallel",)),
        name="inproj",
    )(x2, g, w_bf, gains, bd, cos_t, sa_t, sb_t)
    nb = len(DILATED_PATTERNS)
    return outs[0:nb], outs[nb:2 * nb], outs[2 * nb:3 * nb], outs[3 * nb], outs[3 * nb + 1], outs[3 * nb + 2]


def _dilated_kernel(q_ref, kp_ref, k_ref, vp_ref, v_ref, o_ref, l_ref, kbuf, vbuf, *, tq):
    tile = pl.program_id(2)
    kbuf[0:BLOCK, :] = kp_ref[...]
    kbuf[BLOCK:, :] = k_ref[...]
    vbuf[0:BLOCK, :] = vp_ref[...]
    vbuf[BLOCK:, :] = v_ref[...]

    lane = lax.broadcasted_iota(jnp.int32, (BLOCK, LANES), 1)
    even = lane < HEAD_DIM
    head_masks = (even, lane >= HEAD_DIM)
    qi = lax.broadcasted_iota(jnp.int32, (BLOCK, 2 * BLOCK), 0)
    kj = lax.broadcasted_iota(jnp.int32, (BLOCK, 2 * BLOCK), 1)
    dist = qi + BLOCK - kj
    band = (dist >= 0) & (dist <= BLOCK)

    def body(blk, carry):
        r0 = pl.multiple_of(blk * BLOCK, BLOCK)
        first_key = jnp.where((tile == 0) & (blk == 0), BLOCK, 0)
        valid = band & (kj >= first_key)
        for pair in range(WIDTH_A // LANES):
            cols = slice(pair * LANES, (pair + 1) * LANES)
            q = q_ref[pl.ds(r0, BLOCK), cols]
            kk = kbuf[pl.ds(r0, 2 * BLOCK), cols]
            vv = vbuf[pl.ds(r0, 2 * BLOCK), cols]
            outs, lses = [], []
            for head_mask in head_masks:
                qh = jnp.where(head_mask, q, jnp.zeros_like(q))
                s = lax.dot_general(qh, kk, (((1,), (1,)), ((), ())), preferred_element_type=F32)
                s = jnp.where(valid, s, NEG_INF)
                m = jnp.max(s, axis=-1, keepdims=True)
                p = jnp.exp(s - m)
                den = jnp.sum(p, axis=-1, keepdims=True)
                o = jnp.dot(p.astype(BF16), vv, preferred_element_type=F32) / den
                outs.append(o)
                lses.append(jnp.broadcast_to(m + jnp.log(den), (BLOCK, LANES)))
            o_ref[pl.ds(r0, BLOCK), cols] = jnp.where(even, outs[0], outs[1])
            l_ref[pl.ds(r0, BLOCK), cols] = jnp.where(even, lses[0], lses[1])
        return carry

    lax.fori_loop(0, tq // BLOCK, body, 0)


def _dilated(qa, ka, va):
    batch, dilation, stream, _ = qa.shape
    tq = min(stream, 1024)
    nprev = tq // BLOCK
    cur = lambda b, r, i: (b, r, i, 0)
    prev = lambda b, r, i: (b, r, jnp.maximum(i * nprev - 1, 0), 0)
    blk_cur = pl.BlockSpec((None, None, tq, WIDTH_A), cur)
    blk_prev = pl.BlockSpec((None, None, BLOCK, WIDTH_A), prev)
    out_sds = jax.ShapeDtypeStruct(qa.shape, F32)
    return pl.pallas_call(
        functools.partial(_dilated_kernel, tq=tq),
        grid=(batch, dilation, stream // tq),
        in_specs=[blk_cur, blk_prev, blk_cur, blk_prev, blk_cur],
        out_specs=[blk_cur, blk_cur],
        out_shape=[out_sds, out_sds],
        scratch_shapes=[pltpu.VMEM((tq + BLOCK, WIDTH_A), BF16), pltpu.VMEM((tq + BLOCK, WIDTH_A), BF16)],
        compiler_params=_cparams(("parallel", "parallel", "arbitrary")),
        name=f"dilated_d{dilation}",
    )(qa, ka, ka, va, va)


def _diff_kernel(lq1_ref, lk1_ref, lq2_ref, lk2_ref, gain_ref, qt_ref, k_ref, vt_ref, o_ref,
                 m_sc, l_sc, acc_sc, *, tq):
    i = pl.program_id(2)
    heads = qt_ref.shape[0] // DIFF_V_DIM
    dim = lax.broadcasted_iota(jnp.int32, (DIFF_V_DIM, tq), 0)
    qs = []
    for hh in range(heads):
        qt = qt_ref[hh * DIFF_V_DIM:(hh + 1) * DIFF_V_DIM, :]
        zero = jnp.zeros_like(qt)
        qs += [jnp.where(dim < DIFF_DIM, qt, zero), jnp.where(dim >= DIFF_DIM, qt, zero)]

    m_sc[...] = jnp.full(m_sc.shape, NEG_INF, F32)
    l_sc[...] = jnp.zeros(l_sc.shape, F32)
    acc_sc[...] = jnp.zeros(acc_sc.shape, F32)

    key = lax.broadcasted_iota(jnp.int32, (tq, tq), 0)
    query = lax.broadcasted_iota(jnp.int32, (tq, tq), 1)
    causal = key <= query

    def step(j, masked):
        r0 = pl.multiple_of(j * tq, tq)
        for hh in range(heads):
            cols = slice(hh * DIFF_V_DIM, (hh + 1) * DIFF_V_DIM)
            kb = k_ref[pl.ds(r0, tq), cols]
            vtb = vt_ref[cols, pl.ds(r0, tq)]
            for c in range(2 * hh, 2 * hh + 2):
                s = jnp.dot(kb, qs[c], preferred_element_type=F32)
                if masked:
                    s = jnp.where(causal, s, NEG_INF)
                m_old = m_sc[c]
                m_new = jnp.maximum(m_old, jnp.max(s, axis=0, keepdims=True))
                alpha = jnp.exp(m_old - m_new)
                p = jnp.exp(s - m_new)
                l_sc[c] = alpha * l_sc[c] + jnp.sum(p, axis=0, keepdims=True)
                acc_sc[c] = alpha * acc_sc[c] + jnp.dot(vtb, p.astype(BF16), preferred_element_type=F32)
                m_sc[c] = m_new

    def body(j, carry):
        step(j, False)
        return carry

    lax.fori_loop(0, i, body, 0)
    step(i, True)

    lam = (jnp.exp(jnp.sum(lq1_ref[...] * lk1_ref[...], keepdims=True))
           - jnp.exp(jnp.sum(lq2_ref[...] * lk2_ref[...], keepdims=True)) + LAM_INIT)
    for hh in range(heads):
        cols = slice(hh * DIFF_V_DIM, (hh + 1) * DIFF_V_DIM)
        ot = acc_sc[2 * hh] / l_sc[2 * hh] - lam * (acc_sc[2 * hh + 1] / l_sc[2 * hh + 1])
        o = ot.T
        ms = jnp.mean(o * o, axis=-1, keepdims=True)
        o = o * lax.rsqrt(ms + NORM_EPS) * gain_ref[:, cols] * (1.0 - LAM_INIT)
        o_ref[:, cols] = o.astype(o_ref.dtype)


def _diffattn(qbt, kb, vbt, lams, gain_b, tq=512, heads=2):
    batch, _, seq = qbt.shape
    t = batch * seq
    nq = seq // tq
    width = heads * DIFF_V_DIM
    lam_spec = pl.BlockSpec((1, DIFF_DIM), lambda b, h, i: (0, 0))
    k_spec = pl.BlockSpec((seq, width), lambda b, h, i: (b, h))
    vt_spec = pl.BlockSpec((None, width, seq), lambda b, h, i: (b, h, 0))
    qt_spec = pl.BlockSpec((None, width, tq), lambda b, h, i: (b, h, i))
    o_spec = pl.BlockSpec((tq, width), lambda b, h, i: (b * nq + i, h))
    return pl.pallas_call(
        functools.partial(_diff_kernel, tq=tq),
        grid=(batch, N_HEADS_B // heads, nq),
        in_specs=[lam_spec] * 4 + [pl.BlockSpec((1, width), lambda b, h, i: (0, h)), qt_spec, k_spec, vt_spec],
        out_specs=o_spec,
        out_shape=jax.ShapeDtypeStruct((t, WIDTH_B), BF16),
        scratch_shapes=[pltpu.VMEM((2 * heads, 1, tq), F32), pltpu.VMEM((2 * heads, 1, tq), F32),
                        pltpu.VMEM((2 * heads, DIFF_V_DIM, tq), F32)],
        compiler_params=_cparams(("parallel", "parallel", "arbitrary")),
        name="diffattn",
    )(*lams, gain_b, qbt, kb, vbt)


def _outproj_kernel(x_ref, o1_ref, o2_ref, o3_ref, l1_ref, l2_ref, l3_ref, ob_ref, wa_ref, wb_ref,
                    gna_ref, bd_ref, gffn_ref, wq_ref, sk_ref, h_ref, hn_ref, sc_ref, *order_sc):
    tm = x_ref.shape[0]

    def token_order(ref, sc):
        dil = ref.shape[0]
        if dil == 1:
            return ref[0]
        nslab = WIDTH_A // LANES
        for r in range(dil):
            for k in range(nslab):
                sc[k, pl.ds(r, tm // dil, stride=dil), :] = ref[r, :, k * LANES:(k + 1) * LANES]
        return jnp.concatenate([sc[k] for k in range(nslab)], axis=1)

    o1, o2, o3, l1, l2, l3 = [token_order(ref, sc) for ref, sc in
                              zip((o1_ref, o2_ref, o3_ref, l1_ref, l2_ref, l3_ref), order_sc)]
    lmax = jnp.maximum(jnp.maximum(l1, l2), l3)
    e1, e2, e3 = jnp.exp(l1 - lmax), jnp.exp(l2 - lmax), jnp.exp(l3 - lmax)
    oa = (e1 * o1 + e2 * o2 + e3 * o3) / (e1 + e2 + e3)
    ssum = _group_sum(oa * oa, bd_ref[...])
    oa = oa * lax.rsqrt(ssum * (1.0 / HEAD_DIM) + NORM_EPS) * gna_ref[...]
    h = (x_ref[...]
         + jnp.dot(oa.astype(BF16), wa_ref[...], preferred_element_type=F32)
         + jnp.dot(ob_ref[...], wb_ref[...], preferred_element_type=F32))
    h_ref[...] = h
    ms = jnp.mean(h * h, axis=-1, keepdims=True)
    hn = h * lax.rsqrt(ms + NORM_EPS) * gffn_ref[...]
    hn_ref[...] = hn
    q = jnp.dot(hn.astype(BF16), wq_ref[...], preferred_element_type=F32).astype(BF16)
    for hc in range(2 * PEER_HEADS):
        qs = q[:, hc * KEY_DIM:(hc + 1) * KEY_DIM]
        sc_ref[hc] = lax.dot_general(sk_ref[hc], qs, (((1,), (1,)), ((), ())), preferred_element_type=F32)


def _outproj(x2, o_l, ob, wa, wb, gna, bd, gffn, wq, sk, tm=256):
    t, d = x2.shape
    nseq = (o_l[0].shape[1] * o_l[0].shape[2]) // tm
    row = lambda i: (i, 0)
    fixed = lambda i: (0, 0)
    half = pl.BlockSpec((tm, WIDTH_A), row)
    full = pl.BlockSpec((tm, d), row)
    streams = [pl.BlockSpec((None, a.shape[1], tm // a.shape[1], WIDTH_A),
                            lambda i: (i // nseq, 0, i % nseq, 0)) for a in o_l]
    nq = wq.shape[1]
    return pl.pallas_call(
        _outproj_kernel,
        grid=(t // tm,),
        in_specs=[full] + streams + [half]
                 + [pl.BlockSpec((WIDTH_A, d), fixed), pl.BlockSpec((WIDTH_B, d), fixed),
                    pl.BlockSpec((1, WIDTH_A), fixed), pl.BlockSpec((WIDTH_A, WIDTH_A), fixed),
                    pl.BlockSpec((1, d), fixed), pl.BlockSpec((d, nq), fixed),
                    pl.BlockSpec((2 * PEER_HEADS, N_KEYS, KEY_DIM), lambda i: (0, 0, 0))],
        out_specs=[full, full, pl.BlockSpec((2 * PEER_HEADS, N_KEYS, tm), lambda i: (0, 0, i))],
        out_shape=[jax.ShapeDtypeStruct((t, d), F32), jax.ShapeDtypeStruct((t, d), F32),
                   jax.ShapeDtypeStruct((2 * PEER_HEADS, N_KEYS, t), F32)],
        scratch_shapes=[pltpu.VMEM((WIDTH_A // LANES, tm, LANES), F32)] * len(o_l),
        compiler_params=_cparams(("parallel",)),
        name="outproj",
    )(x2, *o_l, ob, wa, wb, gna, bd, gffn, wq, sk)


def _extract_top(vals, payload, count):
    rows = vals.shape[0]
    iota = lax.broadcasted_iota(jnp.int32, vals.shape, 0).astype(F32)
    tops, picks = [], []
    for _ in range(count):
        m = jnp.max(vals, axis=0, keepdims=True)
        win = jnp.min(jnp.where(vals == m, iota, float(rows)), axis=0, keepdims=True)
        hit = iota == win
        tops.append(m)
        picks.append(win.astype(jnp.int32) if payload is None
                     else jnp.sum(jnp.where(hit, payload, 0), axis=0, keepdims=True))
        vals = jnp.where(hit, NEG_INF, vals)
    return tops, picks


PAIR_COUNTS = tuple(PEER_TOPK // (a + 1) for a in range(PEER_TOPK))
N_PAIRS = sum(PAIR_COUNTS)
PAIR_ROWS = -(-N_PAIRS // 8) * 8


def _topk_kernel(sc_ref, idx_ref, gate_ref, idx_sc, gate_sc, cand_sc, cid_sc):
    tokens = cand_sc.shape[1]
    cand_sc[N_PAIRS:, :] = jnp.full((PAIR_ROWS - N_PAIRS, tokens), NEG_INF, F32)
    cid_sc[N_PAIRS:, :] = jnp.zeros((PAIR_ROWS - N_PAIRS, tokens), jnp.int32)
    for h in range(PEER_HEADS):
        v1, i1 = _extract_top(sc_ref[2 * h], None, PEER_TOPK)
        v2, i2 = _extract_top(sc_ref[2 * h + 1], None, PEER_TOPK)
        v2m = jnp.concatenate(v2, axis=0)
        i2m = jnp.concatenate(i2, axis=0)
        off = 0
        for a, nb in enumerate(PAIR_COUNTS):
            cand_sc[off:off + nb, :] = v1[a] + v2m[0:nb]
            cid_sc[off:off + nb, :] = i1[a] * N_KEYS + i2m[0:nb]
            off += nb
        top, eid = _extract_top(cand_sc[...], cid_sc[...], PEER_TOPK)
        ex = [jnp.exp(tv - top[0]) for tv in top]
        den = ex[0]
        for e in ex[1:]:
            den = den + e
        for k in range(PEER_TOPK):
            j = h * PEER_TOPK + k
            idx_sc[j:j + 1, :] = eid[k]
            gate_sc[j:j + 1, :] = ex[k] / den
    idx_ref[0] = idx_sc[...].T
    gate_ref[0] = gate_sc[...].T


def _topk(scores_t, tt):
    t = scores_t.shape[-1]
    nt = t // tt
    out_spec = pl.BlockSpec((1, tt, PEER_SLOTS), lambda i: (i, 0, 0))
    return pl.pallas_call(
        _topk_kernel,
        grid=(nt,),
        in_specs=[pl.BlockSpec((2 * PEER_HEADS, N_KEYS, tt), lambda i: (0, 0, i))],
        out_specs=[out_spec, out_spec],
        out_shape=[jax.ShapeDtypeStruct((nt, tt, PEER_SLOTS), jnp.int32),
                   jax.ShapeDtypeStruct((nt, tt, PEER_SLOTS), F32)],
        scratch_shapes=[pltpu.VMEM((PEER_SLOTS, tt), jnp.int32), pltpu.VMEM((PEER_SLOTS, tt), F32),
                        pltpu.VMEM((PAIR_ROWS, tt), F32), pltpu.VMEM((PAIR_ROWS, tt), jnp.int32)],
        compiler_params=_cparams(("parallel",)),
        name="topk",
    )(scores_t)


def _for_each_token(idx_ref, idx_bufs, sems, token_fn, tt, group_fn=None):
    group = PEER_GROUP
    nbuf = len(idx_bufs)
    ngroups = tt // group
    ahead = nbuf - 1

    def ids_copy(g, k):
        return pltpu.make_async_copy(idx_ref.at[0, pl.ds(g * group, group)], idx_bufs[k], sems.at[k])

    for g in range(ahead):
        ids_copy(g, g).start()

    def round_(q, carry):
        for k in range(nbuf):
            g = q * nbuf + k
            ids_copy(g, k).wait()

            @pl.when(g + ahead < ngroups)
            def _():
                ids_copy(g + ahead, (k + ahead) % nbuf).start()

            if group_fn is not None:
                group_fn(g)
            for u in range(group):
                token_fn(g * group + u, u, idx_bufs[k].at[u])
        return carry

    lax.fori_loop(0, ngroups // nbuf, round_, 0)


def _id_scratch():
    return [pltpu.SMEM((PEER_GROUP, PEER_SLOTS), jnp.int32)] * PEER_ID_BUFS + [pltpu.SemaphoreType.DMA((PEER_ID_BUFS,))]


def _to_chunk_major(x_ref, chunk_sc, tt):
    for c in range(ROW_CHUNKS):
        chunk_sc[pl.ds(c, tt, stride=ROW_CHUNKS), :] = x_ref[:, c * LANES:(c + 1) * LANES]


def _sublane_sums(tiles):
    sub = lax.broadcasted_iota(jnp.int32, tiles[0].shape, 0)
    n = tiles[0].shape[0]

    def merge(a, b, step):
        first = (sub & step) == 0
        if 2 * step == n:
            return jnp.where(first, a, b) + pltpu.roll(jnp.where(first, b, a), step, 0)
        partner = jnp.where(first, pltpu.roll(a, n - step, 0), pltpu.roll(b, step, 0))
        return jnp.where(first, a, b) + partner

    level = [tiles[k] for k in (0, 4, 2, 6, 1, 5, 3, 7)]
    for step in (4, 2, 1):
        level = [merge(level[i], level[i + 1], step) for i in range(0, len(level), 2)]
    return level[0]


def _peer_dot_kernel(idx_ref, x_ref, tab_ref, gate_ref, act_ref, *scratch, tt):
    idx_bufs, sems, x_sc, z_sc = scratch[:PEER_ID_BUFS], scratch[PEER_ID_BUFS], scratch[-2], scratch[-1]
    _to_chunk_major(x_ref, x_sc, tt)
    z_sc[...] = jnp.zeros(z_sc.shape, F32)
    lane = lax.broadcasted_iota(jnp.int32, z_sc.shape, 1)

    def token(t, u, slot_ids):
        xt = x_sc[pl.ds(pl.multiple_of(t * ROW_CHUNKS, ROW_CHUNKS), ROW_CHUNKS), :]
        cols = []
        for j0 in range(0, PEER_SLOTS, ROW_CHUNKS):
            prods = [tab_ref[slot_ids[j0 + k]].astype(F32) * xt for k in range(ROW_CHUNKS)]
            cols.append(jnp.sum(_sublane_sums(prods), axis=1, keepdims=True))
        col = jnp.concatenate(cols, axis=0)
        z_sc[...] = jnp.where(lane == t, col, z_sc[...])

    _for_each_token(idx_ref, idx_bufs, sems, token, tt)
    z = z_sc[...].T
    gelu = 0.5 * z * (1.0 + lax.erf(z * (1.0 / math.sqrt(2.0))))
    act_ref[0] = gelu * gate_ref[0]


def _peer_dot(idx_t, hn, tab, gate_t, tt):
    nt = idx_t.shape[0]
    d = hn.shape[1]
    tile = lambda i: (i, 0, 0)
    slots = pl.BlockSpec((1, tt, PEER_SLOTS), tile)
    return pl.pallas_call(
        functools.partial(_peer_dot_kernel, tt=tt),
        grid=(nt,),
        in_specs=[slots,
                  pl.BlockSpec((tt, d), lambda i: (i, 0)),
                  pl.BlockSpec(memory_space=pltpu.VMEM),
                  slots],
        out_specs=slots,
        out_shape=jax.ShapeDtypeStruct((nt, tt, PEER_SLOTS), F32),
        scratch_shapes=_id_scratch()
                       + [pltpu.VMEM((tt * ROW_CHUNKS, LANES), F32), pltpu.VMEM((PEER_SLOTS, tt), F32)],
        compiler_params=_cparams(("arbitrary",)),
        name="peer_dot",
    )(idx_t, hn, tab, gate_t)


def _peer_out_kernel(idx_ref, act_ref, tab_ref, h_ref, y_ref, *scratch, tt):
    idx_bufs, sems = scratch[:PEER_ID_BUFS], scratch[PEER_ID_BUFS]
    y_sc, actg_sc, w_sc = scratch[PEER_ID_BUFS + 1:]
    act_t = act_ref[0].T
    for g in range(tt // PEER_GROUP):
        actg_sc[g] = act_t[:, g * PEER_GROUP:(g + 1) * PEER_GROUP]
    nacc = 4

    def group_weights(g):
        act_g = actg_sc[g]
        for u in range(PEER_GROUP):
            w_sc[u * PEER_SLOTS:(u + 1) * PEER_SLOTS, :] = jnp.broadcast_to(act_g[:, u:u + 1], (PEER_SLOTS, LANES))

    def token(t, u, slot_ids):
        acc = [None] * nacc
        for j in range(PEER_SLOTS):
            row = tab_ref[slot_ids[j]].astype(F32)
            r = u * PEER_SLOTS + j
            term = jnp.broadcast_to(w_sc[r:r + 1, :], row.shape) * row
            acc[j % nacc] = term if acc[j % nacc] is None else acc[j % nacc] + term
        y_sc[pl.ds(pl.multiple_of(t * ROW_CHUNKS, ROW_CHUNKS), ROW_CHUNKS), :] = (acc[0] + acc[1]) + (acc[2] + acc[3])

    _for_each_token(idx_ref, idx_bufs, sems, token, tt, group_fn=group_weights)
    for c in range(ROW_CHUNKS):
        cols = slice(c * LANES, (c + 1) * LANES)
        y_ref[:, cols] = h_ref[:, cols] + y_sc[pl.ds(c, tt, stride=ROW_CHUNKS), :]


def _peer_out(idx_t, act_t, tab, h, tt):
    nt = idx_t.shape[0]
    tile = lambda i: (i, 0, 0)
    tok = pl.BlockSpec((tt, h.shape[1]), lambda i: (i, 0))
    return pl.pallas_call(
        functools.partial(_peer_out_kernel, tt=tt),
        grid=(nt,),
        in_specs=[pl.BlockSpec((1, tt, PEER_SLOTS), tile),
                  pl.BlockSpec((1, tt, PEER_SLOTS), tile),
                  pl.BlockSpec(memory_space=pltpu.VMEM), tok],
        out_specs=tok,
        out_shape=jax.ShapeDtypeStruct(h.shape, F32),
        scratch_shapes=_id_scratch()
                       + [pltpu.VMEM((tt * ROW_CHUNKS, LANES), F32),
                          pltpu.VMEM((tt // PEER_GROUP, PEER_SLOTS, PEER_GROUP), F32),
                          pltpu.VMEM((PEER_GROUP * PEER_SLOTS, LANES), F32)],
        compiler_params=_cparams(("arbitrary",)),
        name="peer_out",
    )(idx_t, act_t, tab, h)


def _chunk_table(tab):
    n, d = tab.shape
    return tab.astype(BF16).reshape(n, d // LANES, LANES)


def _rope_tables(seq):
    pos = jnp.arange(seq, dtype=F32)
    inv_freq = ROPE_THETA ** (-jnp.arange(0, ROT_DIM, 2, dtype=F32) / ROT_DIM)
    ang = pos[:, None] * inv_freq[None, :]
    cos, sin = jnp.cos(ang), jnp.sin(ang)
    half = ROT_DIM // 2
    pad1 = jnp.ones((seq, HEAD_DIM - ROT_DIM), F32)
    pad0 = jnp.zeros((seq, HEAD_DIM - ROT_DIM), F32)
    zeros = jnp.zeros((seq, half), F32)
    cos_h = jnp.concatenate([cos, cos, pad1], axis=1)
    sa_h = jnp.concatenate([-sin, zeros, pad0], axis=1)
    sb_h = jnp.concatenate([zeros, sin, pad0], axis=1)
    two = lambda a: jnp.concatenate([a, a], axis=1)
    return two(cos_h), two(sa_h), two(sb_h)


def _layer(x2, batch, seq, attn_norm_g, w_in, q_norm_a, k_norm_a, out_norm_a, q_norm_b, k_norm_b,
           lambda_q1, lambda_k1, lambda_q2, lambda_k2, out_norm_b, w_out, ffn_norm_g, w_peer_q,
           peer_sub_keys, peer_u, peer_v, peer_tt=128):
    tile_heads = lambda g: jnp.tile(g.astype(F32), WIDTH_A // HEAD_DIM)[None, :]
    gains = jnp.concatenate([tile_heads(q_norm_a), tile_heads(k_norm_a),
                             tile_heads(q_norm_b), tile_heads(k_norm_b)], axis=0)
    grp = jnp.arange(WIDTH_A) // HEAD_DIM
    bd = (grp[:, None] == grp[None, :]).astype(BF16)
    cos_t, sa_t, sb_t = _rope_tables(seq)

    qa, ka, va, qbt, kb, vbt = _inproj(x2, attn_norm_g[None, :].astype(F32), w_in.astype(BF16), gains, bd,
                                       cos_t, sa_t, sb_t, batch, seq)
    outs = [_dilated(q, k, v) for q, k, v in zip(qa, ka, va)]
    lams = [v[None, :].astype(F32) for v in (lambda_q1, lambda_k1, lambda_q2, lambda_k2)]
    ob = _diffattn(qbt, kb, vbt, lams, out_norm_b[None, :].astype(F32))

    w_out_bf = w_out.astype(BF16)
    sk = peer_sub_keys.reshape(2 * PEER_HEADS, N_KEYS, KEY_DIM).astype(BF16)
    h, hn, scores_t = _outproj(x2, [o for o, _ in outs] + [l for _, l in outs], ob,
                               w_out_bf[:WIDTH_A], w_out_bf[WIDTH_A:], out_norm_a[None, :].astype(F32), bd,
                               ffn_norm_g[None, :].astype(F32), w_peer_q.astype(BF16), sk)
    idx_t, gate_t = _topk(scores_t, peer_tt)
    act_t = _peer_dot(idx_t, hn, _chunk_table(peer_u), gate_t, peer_tt)
    return _peer_out(idx_t, act_t, _chunk_table(peer_v), h, peer_tt)


def kernel(x, attn_norm_g, w_in, q_norm_a, k_norm_a, out_norm_a, q_norm_b, k_norm_b, lambda_q1, lambda_k1,
           lambda_q2, lambda_k2, out_norm_b, w_out, ffn_norm_g, w_peer_q, peer_sub_keys, peer_u, peer_v):
    batch, seq, d = x.shape
    y = _layer(x.reshape(batch * seq, d), batch, seq, attn_norm_g[0], w_in[0], q_norm_a[0], k_norm_a[0],
               out_norm_a[0], q_norm_b[0], k_norm_b[0], lambda_q1[0], lambda_k1[0], lambda_q2[0],
               lambda_k2[0], out_norm_b[0], w_out[0], ffn_norm_g[0], w_peer_q[0], peer_sub_keys[0],
               peer_u[0], peer_v[0])
    return y.reshape(batch, seq, d)
```

```python
import functools
import math

import jax
import jax.numpy as jnp
from jax import lax
from jax.experimental import pallas as pl
from jax.experimental.pallas import tpu as pltpu

F32 = jnp.float32
BF16 = jnp.bfloat16

HEAD_DIM = 64
N_HEADS_A = 8
WIDTH_A = N_HEADS_A * HEAD_DIM
DILATED_PATTERNS = ((128, 1), (512, 4), (2048, 16))
N_HEADS_B = 4
DIFF_DIM = HEAD_DIM
DIFF_V_DIM = 2 * DIFF_DIM
WIDTH_B = N_HEADS_B * DIFF_V_DIM
ROPE_THETA = 500000.0
ROT_DIM = HEAD_DIM // 4
BLOCK = 128
NORM_EPS = 1e-6
ATTN_SCALE = 1.0 / math.sqrt(HEAD_DIM)
LAM_INIT = 0.8 - 0.6 * math.exp(-0.3 * 0)

PEER_HEADS = 8
N_KEYS = 128
KEY_DIM = 128
PEER_TOPK = 16
PEER_SLOTS = PEER_HEADS * PEER_TOPK
ROW_CHUNKS = 8
PEER_GROUP = 8
PEER_ID_BUFS = 4

LANES = 128
VMEM_LIMIT = 56 * 1024 * 1024

NEG_INF = float("-inf")


def _cparams(sem):
    return pltpu.CompilerParams(dimension_semantics=sem, vmem_limit_bytes=VMEM_LIMIT)


def _split_bf16(a):
    hi = a.astype(BF16)
    lo = (a - hi.astype(F32)).astype(BF16)
    return hi, lo


def _group_sum(sq, bd):
    hi, lo = _split_bf16(sq)
    return (jnp.dot(hi, bd, preferred_element_type=F32)
            + jnp.dot(lo, bd, preferred_element_type=F32))


def _inproj_body(x_ref, g_ref, w_ref, gains_ref, bd_ref, cos_ref, sa_ref, sb_ref,
                 qa_refs, ka_refs, va_refs, qbt_ref, kb_ref, vbt_ref, stream_sc):
    x = x_ref[...]
    ms = jnp.mean(x * x, axis=-1, keepdims=True)
    xn = (x * lax.rsqrt(ms + NORM_EPS) * g_ref[...]).astype(BF16)
    proj = jnp.dot(xn, w_ref[...], preferred_element_type=F32)
    bd = bd_ref[...]
    rep = WIDTH_A // LANES
    cos = jnp.concatenate([cos_ref[...]] * rep, axis=1)
    sa = jnp.concatenate([sa_ref[...]] * rep, axis=1)
    sb = jnp.concatenate([sb_ref[...]] * rep, axis=1)
    half = ROT_DIM // 2

    def norm_rope(z, gain, scale):
        ssum = _group_sum(z * z, bd)
        zn = z * lax.rsqrt(ssum * (1.0 / HEAD_DIM) + NORM_EPS) * gain
        zr = (zn * cos + pltpu.roll(zn, WIDTH_A - half, 1) * sa + pltpu.roll(zn, half, 1) * sb)
        return zr * scale

    w = WIDTH_A
    tm = x.shape[0]
    mixer_a = (norm_rope(proj[:, 0:w], gains_ref[0:1, :], ATTN_SCALE),
               norm_rope(proj[:, w:2 * w], gains_ref[1:2, :], 1.0),
               proj[:, 2 * w:3 * w])
    for z, refs, sc in zip(mixer_a, (qa_refs, ka_refs, va_refs), stream_sc):
        nslab = w // LANES
        for k in range(nslab):
            sc[k] = z[:, k * LANES:(k + 1) * LANES]
        for (_, dil), ref in zip(DILATED_PATTERNS, refs):
            for r in range(dil):
                ref[r] = jnp.concatenate([sc[k, pl.ds(r, tm // dil, stride=dil), :] for k in range(nslab)],
                                         axis=1).astype(BF16)
    qbt_ref[...] = norm_rope(proj[:, 3 * w:4 * w], gains_ref[2:3, :], ATTN_SCALE).T.astype(BF16)
    kb_ref[...] = norm_rope(proj[:, 4 * w:5 * w], gains_ref[3:4, :], 1.0).astype(BF16)
    vbt_ref[...] = proj[:, 5 * w:6 * w].T.astype(BF16)


def _inproj_kernel(x_ref, g_ref, w_ref, gains_ref, bd_ref, cos_ref, sa_ref, sb_ref, *rest):
    nb = len(DILATED_PATTERNS)
    qa_refs, ka_refs, va_refs = rest[0:nb], rest[nb:2 * nb], rest[2 * nb:3 * nb]
    qbt_ref, kb_ref, vbt_ref = rest[3 * nb:3 * nb + 3]
    _inproj_body(x_ref, g_ref, w_ref, gains_ref, bd_ref, cos_ref, sa_ref, sb_ref,
                 qa_refs, ka_refs, va_refs, qbt_ref, kb_ref, vbt_ref, rest[3 * nb + 3:])


def _inproj(x2, g, w_bf, gains, bd, cos_t, sa_t, sb_t, batch, seq, tm=512):
    t, d = x2.shape
    ncol = w_bf.shape[1]
    nseq = seq // tm
    row = lambda i: (i, 0)
    fixed = lambda i: (0, 0)
    pos = lambda i: (i % nseq, 0)
    stream_specs, stream_sds = [], []
    for _, dil in DILATED_PATTERNS:
        stream_specs.append(pl.BlockSpec((None, dil, tm // dil, WIDTH_A), lambda i: (i // nseq, 0, i % nseq, 0)))
        stream_sds.append(jax.ShapeDtypeStruct((batch, dil, seq // dil, WIDTH_A), BF16))
    head_spec = pl.BlockSpec((None, WIDTH_B, tm), lambda i: (i // nseq, 0, i % nseq))
    head_sds = jax.ShapeDtypeStruct((batch, WIDTH_B, seq), BF16)
    outs = pl.pallas_call(
        _inproj_kernel,
        grid=(t // tm,),
        in_specs=[pl.BlockSpec((tm, d), row), pl.BlockSpec((1, d), fixed),
                  pl.BlockSpec((d, ncol), fixed), pl.BlockSpec((4, WIDTH_A), fixed),
                  pl.BlockSpec((WIDTH_A, WIDTH_A), fixed),
                  pl.BlockSpec((tm, LANES), pos), pl.BlockSpec((tm, LANES), pos),
                  pl.BlockSpec((tm, LANES), pos)],
        out_specs=stream_specs * 3 + [head_spec, pl.BlockSpec((tm, WIDTH_B), row), head_spec],
        out_shape=stream_sds * 3 + [head_sds, jax.ShapeDtypeStruct((t, WIDTH_B), BF16), head_sds],
        scratch_shapes=[pltpu.VMEM((WIDTH_A // LANES, tm, LANES), F32)] * 3,
        compiler_params=_cparams(("parallel",)),
        name="inproj",
    )(x2, g, w_bf, gains, bd, cos_t, sa_t, sb_t)
    nb = len(DILATED_PATTERNS)
    return outs[0:nb], outs[nb:2 * nb], outs[2 * nb:3 * nb], outs[3 * nb], outs[3 * nb + 1], outs[3 * nb + 2]


def _dilated_kernel(q_ref, kp_ref, k_ref, vp_ref, v_ref, o_ref, l_ref, kbuf, vbuf, *, tq):
    tile = pl.program_id(2)
    kbuf[0:BLOCK, :] = kp_ref[...]
    kbuf[BLOCK:, :] = k_ref[...]
    vbuf[0:BLOCK, :] = vp_ref[...]
    vbuf[BLOCK:, :] = v_ref[...]

    lane = lax.broadcasted_iota(jnp.int32, (BLOCK, LANES), 1)
    even = lane < HEAD_DIM
    head_masks = (even, lane >= HEAD_DIM)
    qi = lax.broadcasted_iota(jnp.int32, (BLOCK, 2 * BLOCK), 0)
    kj = lax.broadcasted_iota(jnp.int32, (BLOCK, 2 * BLOCK), 1)
    dist = qi + BLOCK - kj
    band = (dist >= 0) & (dist <= BLOCK)

    def one_block(blk):
        r0 = pl.multiple_of(blk * BLOCK, BLOCK)
        first_key = jnp.where((tile == 0) & (blk == 0), BLOCK, 0)
        valid = band & (kj >= first_key)
        for pair in range(WIDTH_A // LANES):
            cols = slice(pair * LANES, (pair + 1) * LANES)
            q = q_ref[pl.ds(r0, BLOCK), cols]
            kk = kbuf[pl.ds(r0, 2 * BLOCK), cols]
            vv = vbuf[pl.ds(r0, 2 * BLOCK), cols]
            outs, lses = [], []
            for head_mask in head_masks:
                qh = jnp.where(head_mask, q, jnp.zeros_like(q))
                s = lax.dot_general(qh, kk, (((1,), (1,)), ((), ())), preferred_element_type=F32)
                s = jnp.where(valid, s, NEG_INF)
                m = jnp.max(s, axis=-1, keepdims=True)
                p = jnp.exp(s - m)
                den = jnp.sum(p, axis=-1, keepdims=True)
                o = jnp.dot(p.astype(BF16), vv, preferred_element_type=F32) / den
                outs.append(o)
                lses.append(jnp.broadcast_to(m + jnp.log(den), (BLOCK, LANES)))
            o_ref[pl.ds(r0, BLOCK), cols] = jnp.where(even, outs[0], outs[1])
            l_ref[pl.ds(r0, BLOCK), cols] = jnp.where(even, lses[0], lses[1])

    def body(pair, carry):
        one_block(2 * pair)
        one_block(2 * pair + 1)
        return carry

    lax.fori_loop(0, tq // (2 * BLOCK), body, 0)


def _dilated(qa, ka, va):
    batch, dilation, stream, _ = qa.shape
    tq = min(stream, 1024)
    nprev = tq // BLOCK
    cur = lambda b, r, i: (b, r, i, 0)
    prev = lambda b, r, i: (b, r, jnp.maximum(i * nprev - 1, 0), 0)
    blk_cur = pl.BlockSpec((None, None, tq, WIDTH_A), cur)
    blk_prev = pl.BlockSpec((None, None, BLOCK, WIDTH_A), prev)
    out_sds = jax.ShapeDtypeStruct(qa.shape, F32)
    return pl.pallas_call(
        functools.partial(_dilated_kernel, tq=tq),
        grid=(batch, dilation, stream // tq),
        in_specs=[blk_cur, blk_prev, blk_cur, blk_prev, blk_cur],
        out_specs=[blk_cur, blk_cur],
        out_shape=[out_sds, out_sds],
        scratch_shapes=[pltpu.VMEM((tq + BLOCK, WIDTH_A), BF16), pltpu.VMEM((tq + BLOCK, WIDTH_A), BF16)],
        compiler_params=_cparams(("parallel", "parallel", "arbitrary")),
        name=f"dilated_d{dilation}",
    )(qa, ka, ka, va, va)


def _diff_kernel(lq1_ref, lk1_ref, lq2_ref, lk2_ref, gain_ref, qt_ref, k_ref, vt_ref, o_ref,
                 m_sc, l_sc, acc_sc, *, tq):
    i = pl.program_id(2)
    heads = qt_ref.shape[0] // DIFF_V_DIM
    dim = lax.broadcasted_iota(jnp.int32, (DIFF_V_DIM, tq), 0)
    qs = []
    for hh in range(heads):
        qt = qt_ref[hh * DIFF_V_DIM:(hh + 1) * DIFF_V_DIM, :]
        zero = jnp.zeros_like(qt)
        qs += [jnp.where(dim < DIFF_DIM, qt, zero), jnp.where(dim >= DIFF_DIM, qt, zero)]

    m_sc[...] = jnp.full(m_sc.shape, NEG_INF, F32)
    l_sc[...] = jnp.zeros(l_sc.shape, F32)
    acc_sc[...] = jnp.zeros(acc_sc.shape, F32)

    key = lax.broadcasted_iota(jnp.int32, (tq, tq), 0)
    query = lax.broadcasted_iota(jnp.int32, (tq, tq), 1)
    causal = key <= query

    def step(j, masked):
        r0 = pl.multiple_of(j * tq, tq)
        for hh in range(heads):
            cols = slice(hh * DIFF_V_DIM, (hh + 1) * DIFF_V_DIM)
            kb = k_ref[pl.ds(r0, tq), cols]
            vtb = vt_ref[cols, pl.ds(r0, tq)]
            for c in range(2 * hh, 2 * hh + 2):
                s = jnp.dot(kb, qs[c], preferred_element_type=F32)
                if masked:
                    s = jnp.where(causal, s, NEG_INF)
                m_old = m_sc[c]
                m_new = jnp.maximum(m_old, jnp.max(s, axis=0, keepdims=True))
                alpha = jnp.exp(m_old - m_new)
                p = jnp.exp(s - m_new)
                l_sc[c] = alpha * l_sc[c] + jnp.sum(p, axis=0, keepdims=True)
                acc_sc[c] = alpha * acc_sc[c] + jnp.dot(vtb, p.astype(BF16), preferred_element_type=F32)
                m_sc[c] = m_new

    def body(j, carry):
        step(j, False)
        return carry

    lax.fori_loop(0, i, body, 0)
    step(i, True)

    lam = (jnp.exp(jnp.sum(lq1_ref[...] * lk1_ref[...], keepdims=True))
           - jnp.exp(jnp.sum(lq2_ref[...] * lk2_ref[...], keepdims=True)) + LAM_INIT)
    for hh in range(heads):
        cols = slice(hh * DIFF_V_DIM, (hh + 1) * DIFF_V_DIM)
        ot = acc_sc[2 * hh] / l_sc[2 * hh] - lam * (acc_sc[2 * hh + 1] / l_sc[2 * hh + 1])
        o = ot.T
        ms = jnp.mean(o * o, axis=-1, keepdims=True)
        o = o * lax.rsqrt(ms + NORM_EPS) * gain_ref[:, cols] * (1.0 - LAM_INIT)
        o_ref[:, cols] = o.astype(o_ref.dtype)


def _diffattn(qbt, kb, vbt, lams, gain_b, tq=512, heads=2):
    batch, _, seq = qbt.shape
    t = batch * seq
    nq = seq // tq
    width = heads * DIFF_V_DIM
    lam_spec = pl.BlockSpec((1, DIFF_DIM), lambda b, h, i: (0, 0))
    k_spec = pl.BlockSpec((seq, width), lambda b, h, i: (b, h))
    vt_spec = pl.BlockSpec((None, width, seq), lambda b, h, i: (b, h, 0))
    qt_spec = pl.BlockSpec((None, width, tq), lambda b, h, i: (b, h, i))
    o_spec = pl.BlockSpec((tq, width), lambda b, h, i: (b * nq + i, h))
    return pl.pallas_call(
        functools.partial(_diff_kernel, tq=tq),
        grid=(batch, N_HEADS_B // heads, nq),
        in_specs=[lam_spec] * 4 + [pl.BlockSpec((1, width), lambda b, h, i: (0, h)), qt_spec, k_spec, vt_spec],
        out_specs=o_spec,
        out_shape=jax.ShapeDtypeStruct((t, WIDTH_B), BF16),
        scratch_shapes=[pltpu.VMEM((2 * heads, 1, tq), F32), pltpu.VMEM((2 * heads, 1, tq), F32),
                        pltpu.VMEM((2 * heads, DIFF_V_DIM, tq), F32)],
        compiler_params=_cparams(("parallel", "parallel", "arbitrary")),
        name="diffattn",
    )(*lams, gain_b, qbt, kb, vbt)


def _outproj_kernel(x_ref, o1_ref, o2_ref, o3_ref, l1_ref, l2_ref, l3_ref, ob_ref, wa_ref, wb_ref,
                    gna_ref, bd_ref, gffn_ref, wq_ref, sk_ref, h_ref, hn_ref, sc_ref, *order_sc):
    tm = x_ref.shape[0]

    def token_order(ref, sc):
        dil = ref.shape[0]
        if dil == 1:
            return ref[0]
        nslab = WIDTH_A // LANES
        for r in range(dil):
            for k in range(nslab):
                sc[k, pl.ds(r, tm // dil, stride=dil), :] = ref[r, :, k * LANES:(k + 1) * LANES]
        return jnp.concatenate([sc[k] for k in range(nslab)], axis=1)

    o1, o2, o3, l1, l2, l3 = [token_order(ref, sc) for ref, sc in
                              zip((o1_ref, o2_ref, o3_ref, l1_ref, l2_ref, l3_ref), order_sc)]
    lmax = jnp.maximum(jnp.maximum(l1, l2), l3)
    e1, e2, e3 = jnp.exp(l1 - lmax), jnp.exp(l2 - lmax), jnp.exp(l3 - lmax)
    oa = (e1 * o1 + e2 * o2 + e3 * o3) / (e1 + e2 + e3)
    ssum = _group_sum(oa * oa, bd_ref[...])
    oa = oa * lax.rsqrt(ssum * (1.0 / HEAD_DIM) + NORM_EPS) * gna_ref[...]
    h = (x_ref[...]
         + jnp.dot(oa.astype(BF16), wa_ref[...], preferred_element_type=F32)
         + jnp.dot(ob_ref[...], wb_ref[...], preferred_element_type=F32))
    h_ref[...] = h
    ms = jnp.mean(h * h, axis=-1, keepdims=True)
    hn = h * lax.rsqrt(ms + NORM_EPS) * gffn_ref[...]
    hn_ref[...] = hn
    q = jnp.dot(hn.astype(BF16), wq_ref[...], preferred_element_type=F32).astype(BF16)
    for hc in range(2 * PEER_HEADS):
        qs = q[:, hc * KEY_DIM:(hc + 1) * KEY_DIM]
        sc_ref[hc] = lax.dot_general(sk_ref[hc], qs, (((1,), (1,)), ((), ())), preferred_element_type=F32)


def _outproj(x2, o_l, ob, wa, wb, gna, bd, gffn, wq, sk, tm=256):
    t, d = x2.shape
    nseq = (o_l[0].shape[1] * o_l[0].shape[2]) // tm
    row = lambda i: (i, 0)
    fixed = lambda i: (0, 0)
    half = pl.BlockSpec((tm, WIDTH_A), row)
    full = pl.BlockSpec((tm, d), row)
    streams = [pl.BlockSpec((None, a.shape[1], tm // a.shape[1], WIDTH_A),
                            lambda i: (i // nseq, 0, i % nseq, 0)) for a in o_l]
    nq = wq.shape[1]
    return pl.pallas_call(
        _outproj_kernel,
        grid=(t // tm,),
        in_specs=[full] + streams + [half]
                 + [pl.BlockSpec((WIDTH_A, d), fixed), pl.BlockSpec((WIDTH_B, d), fixed),
                    pl.BlockSpec((1, WIDTH_A), fixed), pl.BlockSpec((WIDTH_A, WIDTH_A), fixed),
                    pl.BlockSpec((1, d), fixed), pl.BlockSpec((d, nq), fixed),
                    pl.BlockSpec((2 * PEER_HEADS, N_KEYS, KEY_DIM), lambda i: (0, 0, 0))],
        out_specs=[full, full, pl.BlockSpec((2 * PEER_HEADS, N_KEYS, tm), lambda i: (0, 0, i))],
        out_shape=[jax.ShapeDtypeStruct((t, d), F32), jax.ShapeDtypeStruct((t, d), F32),
                   jax.ShapeDtypeStruct((2 * PEER_HEADS, N_KEYS, t), F32)],
        scratch_shapes=[pltpu.VMEM((WIDTH_A // LANES, tm, LANES), F32)] * len(o_l),
        compiler_params=_cparams(("parallel",)),
        name="outproj",
    )(x2, *o_l, ob, wa, wb, gna, bd, gffn, wq, sk)


def _extract_top(vals, payload, count):
    rows = vals.shape[0]
    iota = lax.broadcasted_iota(jnp.int32, vals.shape, 0).astype(F32)
    tops, picks = [], []
    for _ in range(count):
        m = jnp.max(vals, axis=0, keepdims=True)
        win = jnp.min(jnp.where(vals == m, iota, float(rows)), axis=0, keepdims=True)
        hit = iota == win
        tops.append(m)
        picks.append(win.astype(jnp.int32) if payload is None
                     else jnp.sum(jnp.where(hit, payload, 0), axis=0, keepdims=True))
        vals = jnp.where(hit, NEG_INF, vals)
    return tops, picks


PAIR_COUNTS = tuple(PEER_TOPK // (a + 1) for a in range(PEER_TOPK))
N_PAIRS = sum(PAIR_COUNTS)
PAIR_ROWS = -(-N_PAIRS // 8) * 8


def _topk_kernel(sc_ref, idx_ref, gate_ref, idx_sc, gate_sc, cand_sc, cid_sc):
    tokens = cand_sc.shape[1]
    cand_sc[N_PAIRS:, :] = jnp.full((PAIR_ROWS - N_PAIRS, tokens), NEG_INF, F32)
    cid_sc[N_PAIRS:, :] = jnp.zeros((PAIR_ROWS - N_PAIRS, tokens), jnp.int32)
    for h in range(PEER_HEADS):
        v1, i1 = _extract_top(sc_ref[2 * h], None, PEER_TOPK)
        v2, i2 = _extract_top(sc_ref[2 * h + 1], None, PEER_TOPK)
        v2m = jnp.concatenate(v2, axis=0)
        i2m = jnp.concatenate(i2, axis=0)
        off = 0
        for a, nb in enumerate(PAIR_COUNTS):
            cand_sc[off:off + nb, :] = v1[a] + v2m[0:nb]
            cid_sc[off:off + nb, :] = i1[a] * N_KEYS + i2m[0:nb]
            off += nb
        top, eid = _extract_top(cand_sc[...], cid_sc[...], PEER_TOPK)
        ex = [jnp.exp(tv - top[0]) for tv in top]
        den = ex[0]
        for e in ex[1:]:
            den = den + e
        for k in range(PEER_TOPK):
            j = h * PEER_TOPK + k
            idx_sc[j:j + 1, :] = eid[k]
            gate_sc[j:j + 1, :] = ex[k] / den
    idx_ref[0] = idx_sc[...].T
    gate_ref[0] = gate_sc[...].T


def _topk(scores_t, tt):
    t = scores_t.shape[-1]
    nt = t // tt
    out_spec = pl.BlockSpec((1, tt, PEER_SLOTS), lambda i: (i, 0, 0))
    return pl.pallas_call(
        _topk_kernel,
        grid=(nt,),
        in_specs=[pl.BlockSpec((2 * PEER_HEADS, N_KEYS, tt), lambda i: (0, 0, i))],
        out_specs=[out_spec, out_spec],
        out_shape=[jax.ShapeDtypeStruct((nt, tt, PEER_SLOTS), jnp.int32),
                   jax.ShapeDtypeStruct((nt, tt, PEER_SLOTS), F32)],
        scratch_shapes=[pltpu.VMEM((PEER_SLOTS, tt), jnp.int32), pltpu.VMEM((PEER_SLOTS, tt), F32),
                        pltpu.VMEM((PAIR_ROWS, tt), F32), pltpu.VMEM((PAIR_ROWS, tt), jnp.int32)],
        compiler_params=_cparams(("parallel",)),
        name="topk",
    )(scores_t)


def _for_each_token(idx_ref, idx_bufs, sems, token_fn, tt, group_fn=None):
    group = PEER_GROUP
    nbuf = len(idx_bufs)
    ngroups = tt // group
    ahead = nbuf - 1

    def ids_copy(g, k):
        return pltpu.make_async_copy(idx_ref.at[0, pl.ds(g * group, group)], idx_bufs[k], sems.at[k])

    for g in range(ahead):
        ids_copy(g, g).start()

    def round_(q, carry):
        for k in range(nbuf):
            g = q * nbuf + k
            ids_copy(g, k).wait()

            @pl.when(g + ahead < ngroups)
            def _():
                ids_copy(g + ahead, (k + ahead) % nbuf).start()

            if group_fn is not None:
                group_fn(g)
            for u in range(group):
                token_fn(g * group + u, u, idx_bufs[k].at[u])
        return carry

    lax.fori_loop(0, ngroups // nbuf, round_, 0)


def _id_scratch():
    return [pltpu.SMEM((PEER_GROUP, PEER_SLOTS), jnp.int32)] * PEER_ID_BUFS + [pltpu.SemaphoreType.DMA((PEER_ID_BUFS,))]


def _to_chunk_major(x_ref, chunk_sc, tt):
    for c in range(ROW_CHUNKS):
        chunk_sc[pl.ds(c, tt, stride=ROW_CHUNKS), :] = x_ref[:, c * LANES:(c + 1) * LANES]


def _sublane_sums(tiles):
    sub = lax.broadcasted_iota(jnp.int32, tiles[0].shape, 0)
    n = tiles[0].shape[0]

    def merge(a, b, step):
        first = (sub & step) == 0
        if 2 * step == n:
            return jnp.where(first, a, b) + pltpu.roll(jnp.where(first, b, a), step, 0)
        partner = jnp.where(first, pltpu.roll(a, n - step, 0), pltpu.roll(b, step, 0))
        return jnp.where(first, a, b) + partner

    level = [tiles[k] for k in (0, 4, 2, 6, 1, 5, 3, 7)]
    for step in (4, 2, 1):
        level = [merge(level[i], level[i + 1], step) for i in range(0, len(level), 2)]
    return level[0]


def _peer_dot_kernel(idx_ref, x_ref, tab_ref, gate_ref, act_ref, *scratch, tt):
    idx_bufs, sems, x_sc, z_sc = scratch[:PEER_ID_BUFS], scratch[PEER_ID_BUFS], scratch[-2], scratch[-1]
    _to_chunk_major(x_ref, x_sc, tt)
    z_sc[...] = jnp.zeros(z_sc.shape, F32)
    lane = lax.broadcasted_iota(jnp.int32, z_sc.shape, 1)

    def token(t, u, slot_ids):
        xt = x_sc[pl.ds(pl.multiple_of(t * ROW_CHUNKS, ROW_CHUNKS), ROW_CHUNKS), :]
        cols = []
        for j0 in range(0, PEER_SLOTS, ROW_CHUNKS):
            prods = [tab_ref[slot_ids[j0 + k]].astype(F32) * xt for k in range(ROW_CHUNKS)]
            cols.append(jnp.sum(_sublane_sums(prods), axis=1, keepdims=True))
        col = jnp.concatenate(cols, axis=0)
        z_sc[...] = jnp.where(lane == t, col, z_sc[...])

    _for_each_token(idx_ref, idx_bufs, sems, token, tt)
    z = z_sc[...].T
    gelu = 0.5 * z * (1.0 + lax.erf(z * (1.0 / math.sqrt(2.0))))
    act_ref[0] = gelu * gate_ref[0]


def _peer_dot(idx_t, hn, tab, gate_t, tt):
    nt = idx_t.shape[0]
    d = hn.shape[1]
    tile = lambda i: (i, 0, 0)
    slots = pl.BlockSpec((1, tt, PEER_SLOTS), tile)
    return pl.pallas_call(
        functools.partial(_peer_dot_kernel, tt=tt),
        grid=(nt,),
        in_specs=[slots,
                  pl.BlockSpec((tt, d), lambda i: (i, 0)),
                  pl.BlockSpec(memory_space=pltpu.VMEM),
                  slots],
        out_specs=slots,
        out_shape=jax.ShapeDtypeStruct((nt, tt, PEER_SLOTS), F32),
        scratch_shapes=_id_scratch()
                       + [pltpu.VMEM((tt * ROW_CHUNKS, LANES), F32), pltpu.VMEM((PEER_SLOTS, tt), F32)],
        compiler_params=_cparams(("arbitrary",)),
        name="peer_dot",
    )(idx_t, hn, tab, gate_t)


def _peer_out_kernel(idx_ref, act_ref, tab_ref, h_ref, y_ref, *scratch, tt):
    idx_bufs, sems = scratch[:PEER_ID_BUFS], scratch[PEER_ID_BUFS]
    y_sc, actg_sc, w_sc = scratch[PEER_ID_BUFS + 1:]
    act_t = act_ref[0].T
    for g in range(tt // PEER_GROUP):
        actg_sc[g] = act_t[:, g * PEER_GROUP:(g + 1) * PEER_GROUP]
    nacc = 4

    def group_weights(g):
        act_g = actg_sc[g]
        for u in range(PEER_GROUP):
            w_sc[u * PEER_SLOTS:(u + 1) * PEER_SLOTS, :] = jnp.broadcast_to(act_g[:, u:u + 1], (PEER_SLOTS, LANES))

    def token(t, u, slot_ids):
        acc = [None] * nacc
        for j in range(PEER_SLOTS):
            row = tab_ref[slot_ids[j]].astype(F32)
            r = u * PEER_SLOTS + j
            term = jnp.broadcast_to(w_sc[r:r + 1, :], row.shape) * row
            acc[j % nacc] = term if acc[j % nacc] is None else acc[j % nacc] + term
        y_sc[pl.ds(pl.multiple_of(t * ROW_CHUNKS, ROW_CHUNKS), ROW_CHUNKS), :] = (acc[0] + acc[1]) + (acc[2] + acc[3])

    _for_each_token(idx_ref, idx_bufs, sems, token, tt, group_fn=group_weights)
    for c in range(ROW_CHUNKS):
        cols = slice(c * LANES, (c + 1) * LANES)
        y_ref[:, cols] = h_ref[:, cols] + y_sc[pl.ds(c, tt, stride=ROW_CHUNKS), :]


def _peer_out(idx_t, act_t, tab, h, tt):
    nt = idx_t.shape[0]
    tile = lambda i: (i, 0, 0)
    tok = pl.BlockSpec((tt, h.shape[1]), lambda i: (i, 0))
    return pl.pallas_call(
        functools.partial(_peer_out_kernel, tt=tt),
        grid=(nt,),
        in_specs=[pl.BlockSpec((1, tt, PEER_SLOTS), tile),
                  pl.BlockSpec((1, tt, PEER_SLOTS), tile),
                  pl.BlockSpec(memory_space=pltpu.VMEM), tok],
        out_specs=tok,
        out_shape=jax.ShapeDtypeStruct(h.shape, F32),
        scratch_shapes=_id_scratch()
                       + [pltpu.VMEM((tt * ROW_CHUNKS, LANES), F32),
                          pltpu.VMEM((tt // PEER_GROUP, PEER_SLOTS, PEER_GROUP), F32),
                          pltpu.VMEM((PEER_GROUP * PEER_SLOTS, LANES), F32)],
        compiler_params=_cparams(("arbitrary",)),
        name="peer_out",
    )(idx_t, act_t, tab, h)


def _chunk_table(tab):
    n, d = tab.shape
    return tab.astype(BF16).reshape(n, d // LANES, LANES)


def _rope_tables(seq):
    pos = jnp.arange(seq, dtype=F32)
    inv_freq = ROPE_THETA ** (-jnp.arange(0, ROT_DIM, 2, dtype=F32) / ROT_DIM)
    ang = pos[:, None] * inv_freq[None, :]
    cos, sin = jnp.cos(ang), jnp.sin(ang)
    half = ROT_DIM // 2
    pad1 = jnp.ones((seq, HEAD_DIM - ROT_DIM), F32)
    pad0 = jnp.zeros((seq, HEAD_DIM - ROT_DIM), F32)
    zeros = jnp.zeros((seq, half), F32)
    cos_h = jnp.concatenate([cos, cos, pad1], axis=1)
    sa_h = jnp.concatenate([-sin, zeros, pad0], axis=1)
    sb_h = jnp.concatenate([zeros, sin, pad0], axis=1)
    two = lambda a: jnp.concatenate([a, a], axis=1)
    return two(cos_h), two(sa_h), two(sb_h)


def _layer(x2, batch, seq, attn_norm_g, w_in, q_norm_a, k_norm_a, out_norm_a, q_norm_b, k_norm_b,
           lambda_q1, lambda_k1, lambda_q2, lambda_k2, out_norm_b, w_out, ffn_norm_g, w_peer_q,
           peer_sub_keys, peer_u, peer_v, peer_tt=128):
    tile_heads = lambda g: jnp.tile(g.astype(F32), WIDTH_A // HEAD_DIM)[None, :]
    gains = jnp.concatenate([tile_heads(q_norm_a), tile_heads(k_norm_a),
                             tile_heads(q_norm_b), tile_heads(k_norm_b)], axis=0)
    grp = jnp.arange(WIDTH_A) // HEAD_DIM
    bd = (grp[:, None] == grp[None, :]).astype(BF16)
    cos_t, sa_t, sb_t = _rope_tables(seq)

    qa, ka, va, qbt, kb, vbt = _inproj(x2, attn_norm_g[None, :].astype(F32), w_in.astype(BF16), gains, bd,
                                       cos_t, sa_t, sb_t, batch, seq)
    outs = [_dilated(q, k, v) for q, k, v in zip(qa, ka, va)]
    lams = [v[None, :].astype(F32) for v in (lambda_q1, lambda_k1, lambda_q2, lambda_k2)]
    ob = _diffattn(qbt, kb, vbt, lams, out_norm_b[None, :].astype(F32))

    w_out_bf = w_out.astype(BF16)
    sk = peer_sub_keys.reshape(2 * PEER_HEADS, N_KEYS, KEY_DIM).astype(BF16)
    h, hn, scores_t = _outproj(x2, [o for o, _ in outs] + [l for _, l in outs], ob,
                               w_out_bf[:WIDTH_A], w_out_bf[WIDTH_A:], out_norm_a[None, :].astype(F32), bd,
                               ffn_norm_g[None, :].astype(F32), w_peer_q.astype(BF16), sk)
    idx_t, gate_t = _topk(scores_t, peer_tt)
    act_t = _peer_dot(idx_t, hn, _chunk_table(peer_u), gate_t, peer_tt)
    return _peer_out(idx_t, act_t, _chunk_table(peer_v), h, peer_tt)


def kernel(x, attn_norm_g, w_in, q_norm_a, k_norm_a, out_norm_a, q_norm_b, k_norm_b, lambda_q1, lambda_k1,
           lambda_q2, lambda_k2, out_norm_b, w_out, ffn_norm_g, w_peer_q, peer_sub_keys, peer_u, peer_v):
    batch, seq, d = x.shape
    y = _layer(x.reshape(batch * seq, d), batch, seq, attn_norm_g[0], w_in[0], q_norm_a[0], k_norm_a[0],
               out_norm_a[0], q_norm_b[0], k_norm_b[0], lambda_q1[0], lambda_k1[0], lambda_q2[0],
               lambda_k2[0], out_norm_b[0], w_out[0], ffn_norm_g[0], w_peer_q[0], peer_sub_keys[0],
               peer_u[0], peer_v[0])
    return y.reshape(batch, seq, d)
```

```python
import functools
import math

import jax
import jax.numpy as jnp
from jax import lax
from jax.experimental import pallas as pl
from jax.experimental.pallas import tpu as pltpu

F32 = jnp.float32
BF16 = jnp.bfloat16

HEAD_DIM = 64
N_HEADS_A = 8
WIDTH_A = N_HEADS_A * HEAD_DIM
DILATED_PATTERNS = ((128, 1), (512, 4), (2048, 16))
N_HEADS_B = 4
DIFF_DIM = HEAD_DIM
DIFF_V_DIM = 2 * DIFF_DIM
WIDTH_B = N_HEADS_B * DIFF_V_DIM
ROPE_THETA = 500000.0
ROT_DIM = HEAD_DIM // 4
BLOCK = 128
NORM_EPS = 1e-6
ATTN_SCALE = 1.0 / math.sqrt(HEAD_DIM)
LAM_INIT = 0.8 - 0.6 * math.exp(-0.3 * 0)

PEER_HEADS = 8
N_KEYS = 128
KEY_DIM = 128
PEER_TOPK = 16
PEER_SLOTS = PEER_HEADS * PEER_TOPK
ROW_CHUNKS = 8
PEER_GROUP = 8
PEER_ID_BUFS = 4

LANES = 128
VMEM_LIMIT = 56 * 1024 * 1024

NEG_INF = float("-inf")


def _cparams(sem):
    return pltpu.CompilerParams(dimension_semantics=sem, vmem_limit_bytes=VMEM_LIMIT)


def _split_bf16(a):
    hi = a.astype(BF16)
    lo = (a - hi.astype(F32)).astype(BF16)
    return hi, lo


def _group_sum(sq, bd):
    hi, lo = _split_bf16(sq)
    return (jnp.dot(hi, bd, preferred_element_type=F32)
            + jnp.dot(lo, bd, preferred_element_type=F32))


def _inproj_body(x_ref, g_ref, w_ref, gains_ref, bd_ref, cos_ref, sa_ref, sb_ref,
                 qa_refs, ka_refs, va_refs, qbt_ref, kb_ref, vbt_ref, stream_sc):
    x = x_ref[...]
    ms = jnp.mean(x * x, axis=-1, keepdims=True)
    xn = (x * lax.rsqrt(ms + NORM_EPS) * g_ref[...]).astype(BF16)
    proj = jnp.dot(xn, w_ref[...], preferred_element_type=F32)
    bd = bd_ref[...]
    rep = WIDTH_A // LANES
    cos = jnp.concatenate([cos_ref[...]] * rep, axis=1)
    sa = jnp.concatenate([sa_ref[...]] * rep, axis=1)
    sb = jnp.concatenate([sb_ref[...]] * rep, axis=1)
    half = ROT_DIM // 2

    def norm_rope(z, gain, scale):
        ssum = _group_sum(z * z, bd)
        zn = z * lax.rsqrt(ssum * (1.0 / HEAD_DIM) + NORM_EPS) * gain
        zr = (zn * cos + pltpu.roll(zn, WIDTH_A - half, 1) * sa + pltpu.roll(zn, half, 1) * sb)
        return zr * scale

    w = WIDTH_A
    tm = x.shape[0]
    mixer_a = (norm_rope(proj[:, 0:w], gains_ref[0:1, :], ATTN_SCALE),
               norm_rope(proj[:, w:2 * w], gains_ref[1:2, :], 1.0),
               proj[:, 2 * w:3 * w])
    for z, refs, sc in zip(mixer_a, (qa_refs, ka_refs, va_refs), stream_sc):
        nslab = w // LANES
        for k in range(nslab):
            sc[k] = z[:, k * LANES:(k + 1) * LANES]
        for (_, dil), ref in zip(DILATED_PATTERNS, refs):
            for r in range(dil):
                ref[r] = jnp.concatenate([sc[k, pl.ds(r, tm // dil, stride=dil), :] for k in range(nslab)],
                                         axis=1).astype(BF16)
    qbt_ref[...] = norm_rope(proj[:, 3 * w:4 * w], gains_ref[2:3, :], ATTN_SCALE).T.astype(BF16)
    kb_ref[...] = norm_rope(proj[:, 4 * w:5 * w], gains_ref[3:4, :], 1.0).astype(BF16)
    vbt_ref[...] = proj[:, 5 * w:6 * w].T.astype(BF16)


def _inproj_kernel(x_ref, g_ref, w_ref, gains_ref, bd_ref, cos_ref, sa_ref, sb_ref, *rest):
    nb = len(DILATED_PATTERNS)
    qa_refs, ka_refs, va_refs = rest[0:nb], rest[nb:2 * nb], rest[2 * nb:3 * nb]
    qbt_ref, kb_ref, vbt_ref = rest[3 * nb:3 * nb + 3]
    _inproj_body(x_ref, g_ref, w_ref, gains_ref, bd_ref, cos_ref, sa_ref, sb_ref,
                 qa_refs, ka_refs, va_refs, qbt_ref, kb_ref, vbt_ref, rest[3 * nb + 3:])


def _inproj(x2, g, w_bf, gains, bd, cos_t, sa_t, sb_t, batch, seq, tm=512):
    t, d = x2.shape
    ncol = w_bf.shape[1]
    nseq = seq // tm
    row = lambda i: (i, 0)
    fixed = lambda i: (0, 0)
    pos = lambda i: (i % nseq, 0)
    stream_specs, stream_sds = [], []
    for _, dil in DILATED_PATTERNS:
        stream_specs.append(pl.BlockSpec((None, dil, tm // dil, WIDTH_A), lambda i: (i // nseq, 0, i % nseq, 0)))
        stream_sds.append(jax.ShapeDtypeStruct((batch, dil, seq // dil, WIDTH_A), BF16))
    head_spec = pl.BlockSpec((None, WIDTH_B, tm), lambda i: (i // nseq, 0, i % nseq))
    head_sds = jax.ShapeDtypeStruct((batch, WIDTH_B, seq), BF16)
    outs = pl.pallas_call(
        _inproj_kernel,
        grid=(t // tm,),
        in_specs=[pl.BlockSpec((tm, d), row), pl.BlockSpec((1, d), fixed),
                  pl.BlockSpec((d, ncol), fixed), pl.BlockSpec((4, WIDTH_A), fixed),
                  pl.BlockSpec((WIDTH_A, WIDTH_A), fixed),
                  pl.BlockSpec((tm, LANES), pos), pl.BlockSpec((tm, LANES), pos),
                  pl.BlockSpec((tm, LANES), pos)],
        out_specs=stream_specs * 3 + [head_spec, pl.BlockSpec((tm, WIDTH_B), row), head_spec],
        out_shape=stream_sds * 3 + [head_sds, jax.ShapeDtypeStruct((t, WIDTH_B), BF16), head_sds],
        scratch_shapes=[pltpu.VMEM((WIDTH_A // LANES, tm, LANES), F32)] * 3,
        compiler_params=_cparams(("parallel",)),
        name="inproj",
    )(x2, g, w_bf, gains, bd, cos_t, sa_t, sb_t)
    nb = len(DILATED_PATTERNS)
    return outs[0:nb], outs[nb:2 * nb], outs[2 * nb:3 * nb], outs[3 * nb], outs[3 * nb + 1], outs[3 * nb + 2]


def _dilated_kernel(q_ref, kp_ref, k_ref, vp_ref, v_ref, o_ref, l_ref, kbuf, vbuf, *, tq):
    tile = pl.program_id(2)
    kbuf[0:BLOCK, :] = kp_ref[...]
    kbuf[BLOCK:, :] = k_ref[...]
    vbuf[0:BLOCK, :] = vp_ref[...]
    vbuf[BLOCK:, :] = v_ref[...]

    lane = lax.broadcasted_iota(jnp.int32, (BLOCK, LANES), 1)
    even = lane < HEAD_DIM
    head_masks = (even, lane >= HEAD_DIM)
    qi = lax.broadcasted_iota(jnp.int32, (BLOCK, 2 * BLOCK), 0)
    kj = lax.broadcasted_iota(jnp.int32, (BLOCK, 2 * BLOCK), 1)
    dist = qi + BLOCK - kj
    band = (dist >= 0) & (dist <= BLOCK)

    def one_block(blk):
        r0 = pl.multiple_of(blk * BLOCK, BLOCK)
        first_key = jnp.where((tile == 0) & (blk == 0), BLOCK, 0)
        valid = band & (kj >= first_key)
        for pair in range(WIDTH_A // LANES):
            cols = slice(pair * LANES, (pair + 1) * LANES)
            q = q_ref[pl.ds(r0, BLOCK), cols]
            kk = kbuf[pl.ds(r0, 2 * BLOCK), cols]
            vv = vbuf[pl.ds(r0, 2 * BLOCK), cols]
            outs, lses = [], []
            for head_mask in head_masks:
                qh = jnp.where(head_mask, q, jnp.zeros_like(q))
                s = lax.dot_general(qh, kk, (((1,), (1,)), ((), ())), preferred_element_type=F32)
                s = jnp.where(valid, s, NEG_INF)
                m = jnp.max(s, axis=-1, keepdims=True)
                p = jnp.exp(s - m)
                den = jnp.sum(p, axis=-1, keepdims=True)
                o = jnp.dot(p.astype(BF16), vv, preferred_element_type=F32) / den
                outs.append(o)
                lses.append(jnp.broadcast_to(m + jnp.log(den), (BLOCK, LANES)))
            o_ref[pl.ds(r0, BLOCK), cols] = jnp.where(even, outs[0], outs[1])
            l_ref[pl.ds(r0, BLOCK), cols] = jnp.where(even, lses[0], lses[1])

    def body(pair, carry):
        one_block(2 * pair)
        one_block(2 * pair + 1)
        return carry

    lax.fori_loop(0, tq // (2 * BLOCK), body, 0)


def _dilated(qa, ka, va):
    batch, dilation, stream, _ = qa.shape
    tq = min(stream, 1024)
    nprev = tq // BLOCK
    cur = lambda b, r, i: (b, r, i, 0)
    prev = lambda b, r, i: (b, r, jnp.maximum(i * nprev - 1, 0), 0)
    blk_cur = pl.BlockSpec((None, None, tq, WIDTH_A), cur)
    blk_prev = pl.BlockSpec((None, None, BLOCK, WIDTH_A), prev)
    out_sds = jax.ShapeDtypeStruct(qa.shape, F32)
    return pl.pallas_call(
        functools.partial(_dilated_kernel, tq=tq),
        grid=(batch, dilation, stream // tq),
        in_specs=[blk_cur, blk_prev, blk_cur, blk_prev, blk_cur],
        out_specs=[blk_cur, blk_cur],
        out_shape=[out_sds, out_sds],
        scratch_shapes=[pltpu.VMEM((tq + BLOCK, WIDTH_A), BF16), pltpu.VMEM((tq + BLOCK, WIDTH_A), BF16)],
        compiler_params=_cparams(("parallel", "parallel", "arbitrary")),
        name=f"dilated_d{dilation}",
    )(qa, ka, ka, va, va)


def _diff_kernel(lq1_ref, lk1_ref, lq2_ref, lk2_ref, gain_ref, qt_ref, k_ref, vt_ref, o_ref,
                 m_sc, l_sc, acc_sc, *, tq):
    i = pl.program_id(2)
    heads = qt_ref.shape[0] // DIFF_V_DIM
    dim = lax.broadcasted_iota(jnp.int32, (DIFF_V_DIM, tq), 0)
    qs = []
    for hh in range(heads):
        qt = qt_ref[hh * DIFF_V_DIM:(hh + 1) * DIFF_V_DIM, :]
        zero = jnp.zeros_like(qt)
        qs += [jnp.where(dim < DIFF_DIM, qt, zero), jnp.where(dim >= DIFF_DIM, qt, zero)]

    m_sc[...] = jnp.full(m_sc.shape, NEG_INF, F32)
    l_sc[...] = jnp.zeros(l_sc.shape, F32)
    acc_sc[...] = jnp.zeros(acc_sc.shape, F32)

    key = lax.broadcasted_iota(jnp.int32, (tq, tq), 0)
    query = lax.broadcasted_iota(jnp.int32, (tq, tq), 1)
    causal = key <= query

    def step(j, masked):
        r0 = pl.multiple_of(j * tq, tq)
        for hh in range(heads):
            cols = slice(hh * DIFF_V_DIM, (hh + 1) * DIFF_V_DIM)
            kb = k_ref[pl.ds(r0, tq), cols]
            vtb = vt_ref[cols, pl.ds(r0, tq)]
            for c in range(2 * hh, 2 * hh + 2):
                s = jnp.dot(kb, qs[c], preferred_element_type=F32)
                if masked:
                    s = jnp.where(causal, s, NEG_INF)
                m_old = m_sc[c]
                m_new = jnp.maximum(m_old, jnp.max(s, axis=0, keepdims=True))
                alpha = jnp.exp(m_old - m_new)
                p = jnp.exp(s - m_new)
                l_sc[c] = alpha * l_sc[c] + jnp.sum(p, axis=0, keepdims=True)
                acc_sc[c] = alpha * acc_sc[c] + jnp.dot(vtb, p.astype(BF16), preferred_element_type=F32)
                m_sc[c] = m_new

    def body(j, carry):
        step(j, False)
        return carry

    lax.fori_loop(0, i, body, 0)
    step(i, True)

    lam = (jnp.exp(jnp.sum(lq1_ref[...] * lk1_ref[...], keepdims=True))
           - jnp.exp(jnp.sum(lq2_ref[...] * lk2_ref[...], keepdims=True)) + LAM_INIT)
    for hh in range(heads):
        cols = slice(hh * DIFF_V_DIM, (hh + 1) * DIFF_V_DIM)
        ot = acc_sc[2 * hh] / l_sc[2 * hh] - lam * (acc_sc[2 * hh + 1] / l_sc[2 * hh + 1])
        o = ot.T
        ms = jnp.mean(o * o, axis=-1, keepdims=True)
        o = o * lax.rsqrt(ms + NORM_EPS) * gain_ref[:, cols] * (1.0 - LAM_INIT)
        o_ref[:, cols] = o.astype(o_ref.dtype)


def _diffattn(qbt, kb, vbt, lams, gain_b, tq=512, heads=2):
    batch, _, seq = qbt.shape
    t = batch * seq
    nq = seq // tq
    width = heads * DIFF_V_DIM
    lam_spec = pl.BlockSpec((1, DIFF_DIM), lambda b, h, i: (0, 0))
    k_spec = pl.BlockSpec((seq, width), lambda b, h, i: (b, h))
    vt_spec = pl.BlockSpec((None, width, seq), lambda b, h, i: (b, h, 0))
    qt_spec = pl.BlockSpec((None, width, tq), lambda b, h, i: (b, h, i))
    o_spec = pl.BlockSpec((tq, width), lambda b, h, i: (b * nq + i, h))
    return pl.pallas_call(
        functools.partial(_diff_kernel, tq=tq),
        grid=(batch, N_HEADS_B // heads, nq),
        in_specs=[lam_spec] * 4 + [pl.BlockSpec((1, width), lambda b, h, i: (0, h)), qt_spec, k_spec, vt_spec],
        out_specs=o_spec,
        out_shape=jax.ShapeDtypeStruct((t, WIDTH_B), BF16),
        scratch_shapes=[pltpu.VMEM((2 * heads, 1, tq), F32), pltpu.VMEM((2 * heads, 1, tq), F32),
                        pltpu.VMEM((2 * heads, DIFF_V_DIM, tq), F32)],
        compiler_params=_cparams(("parallel", "parallel", "arbitrary")),
        name="diffattn",
    )(*lams, gain_b, qbt, kb, vbt)


def _outproj_kernel(x_ref, o1_ref, o2_ref, o3_ref, l1_ref, l2_ref, l3_ref, ob_ref, wa_ref, wb_ref,
                    gna_ref, bd_ref, gffn_ref, wq_ref, sk_ref, h_ref, hn_ref, sc_ref, *order_sc):
    tm = x_ref.shape[0]

    def token_order(ref, sc):
        dil = ref.shape[0]
        if dil == 1:
            return ref[0]
        nslab = WIDTH_A // LANES
        for r in range(dil):
            for k in range(nslab):
                sc[k, pl.ds(r, tm // dil, stride=dil), :] = ref[r, :, k * LANES:(k + 1) * LANES]
        return jnp.concatenate([sc[k] for k in range(nslab)], axis=1)

    o1, o2, o3, l1, l2, l3 = [token_order(ref, sc) for ref, sc in
                              zip((o1_ref, o2_ref, o3_ref, l1_ref, l2_ref, l3_ref), order_sc)]
    lmax = jnp.maximum(jnp.maximum(l1, l2), l3)
    e1, e2, e3 = jnp.exp(l1 - lmax), jnp.exp(l2 - lmax), jnp.exp(l3 - lmax)
    oa = (e1 * o1 + e2 * o2 + e3 * o3) / (e1 + e2 + e3)
    ssum = _group_sum(oa * oa, bd_ref[...])
    oa = oa * lax.rsqrt(ssum * (1.0 / HEAD_DIM) + NORM_EPS) * gna_ref[...]
    h = (x_ref[...]
         + jnp.dot(oa.astype(BF16), wa_ref[...], preferred_element_type=F32)
         + jnp.dot(ob_ref[...], wb_ref[...], preferred_element_type=F32))
    h_ref[...] = h
    ms = jnp.mean(h * h, axis=-1, keepdims=True)
    hn = h * lax.rsqrt(ms + NORM_EPS) * gffn_ref[...]
    hn_ref[...] = hn
    q = jnp.dot(hn.astype(BF16), wq_ref[...], preferred_element_type=F32).astype(BF16)
    for hc in range(2 * PEER_HEADS):
        qs = q[:, hc * KEY_DIM:(hc + 1) * KEY_DIM]
        sc_ref[hc] = lax.dot_general(sk_ref[hc], qs, (((1,), (1,)), ((), ())), preferred_element_type=F32)


def _outproj(x2, o_l, ob, wa, wb, gna, bd, gffn, wq, sk, tm=256):
    t, d = x2.shape
    nseq = (o_l[0].shape[1] * o_l[0].shape[2]) // tm
    row = lambda i: (i, 0)
    fixed = lambda i: (0, 0)
    half = pl.BlockSpec((tm, WIDTH_A), row)
    full = pl.BlockSpec((tm, d), row)
    streams = [pl.BlockSpec((None, a.shape[1], tm // a.shape[1], WIDTH_A),
                            lambda i: (i // nseq, 0, i % nseq, 0)) for a in o_l]
    nq = wq.shape[1]
    return pl.pallas_call(
        _outproj_kernel,
        grid=(t // tm,),
        in_specs=[full] + streams + [half]
                 + [pl.BlockSpec((WIDTH_A, d), fixed), pl.BlockSpec((WIDTH_B, d), fixed),
                    pl.BlockSpec((1, WIDTH_A), fixed), pl.BlockSpec((WIDTH_A, WIDTH_A), fixed),
                    pl.BlockSpec((1, d), fixed), pl.BlockSpec((d, nq), fixed),
                    pl.BlockSpec((2 * PEER_HEADS, N_KEYS, KEY_DIM), lambda i: (0, 0, 0))],
        out_specs=[full, full, pl.BlockSpec((2 * PEER_HEADS, N_KEYS, tm), lambda i: (0, 0, i))],
        out_shape=[jax.ShapeDtypeStruct((t, d), F32), jax.ShapeDtypeStruct((t, d), F32),
                   jax.ShapeDtypeStruct((2 * PEER_HEADS, N_KEYS, t), F32)],
        scratch_shapes=[pltpu.VMEM((WIDTH_A // LANES, tm, LANES), F32)] * len(o_l),
        compiler_params=_cparams(("parallel",)),
        name="outproj",
    )(x2, *o_l, ob, wa, wb, gna, bd, gffn, wq, sk)


def _extract_top(vals, payload, count):
    rows = vals.shape[0]
    iota = lax.broadcasted_iota(jnp.int32, vals.shape, 0).astype(F32)
    tops, picks = [], []
    for _ in range(count):
        m = jnp.max(vals, axis=0, keepdims=True)
        win = jnp.min(jnp.where(vals == m, iota, float(rows)), axis=0, keepdims=True)
        hit = iota == win
        tops.append(m)
        picks.append(win.astype(jnp.int32) if payload is None
                     else jnp.sum(jnp.where(hit, payload, 0), axis=0, keepdims=True))
        vals = jnp.where(hit, NEG_INF, vals)
    return tops, picks


PAIR_COUNTS = tuple(PEER_TOPK // (a + 1) for a in range(PEER_TOPK))
N_PAIRS = sum(PAIR_COUNTS)
PAIR_ROWS = -(-N_PAIRS // 8) * 8


def _topk_kernel(sc_ref, idx_ref, gate_ref, idx_sc, gate_sc, cand_sc, cid_sc):
    tokens = cand_sc.shape[1]
    cand_sc[N_PAIRS:, :] = jnp.full((PAIR_ROWS - N_PAIRS, tokens), NEG_INF, F32)
    cid_sc[N_PAIRS:, :] = jnp.zeros((PAIR_ROWS - N_PAIRS, tokens), jnp.int32)
    for h in range(PEER_HEADS):
        v1, i1 = _extract_top(sc_ref[2 * h], None, PEER_TOPK)
        v2, i2 = _extract_top(sc_ref[2 * h + 1], None, PEER_TOPK)
        v2m = jnp.concatenate(v2, axis=0)
        i2m = jnp.concatenate(i2, axis=0)
        off = 0
        for a, nb in enumerate(PAIR_COUNTS):
            cand_sc[off:off + nb, :] = v1[a] + v2m[0:nb]
            cid_sc[off:off + nb, :] = i1[a] * N_KEYS + i2m[0:nb]
            off += nb
        top, eid = _extract_top(cand_sc[...], cid_sc[...], PEER_TOPK)
        ex = [jnp.exp(tv - top[0]) for tv in top]
        den = ex[0]
        for e in ex[1:]:
            den = den + e
        for k in range(PEER_TOPK):
            j = h * PEER_TOPK + k
            idx_sc[j:j + 1, :] = eid[k]
            gate_sc[j:j + 1, :] = ex[k] / den
    idx_ref[0] = idx_sc[...].T
    gate_ref[0] = gate_sc[...].T


def _topk(scores_t, tt):
    t = scores_t.shape[-1]
    nt = t // tt
    out_spec = pl.BlockSpec((1, tt, PEER_SLOTS), lambda i: (i, 0, 0))
    return pl.pallas_call(
        _topk_kernel,
        grid=(nt,),
        in_specs=[pl.BlockSpec((2 * PEER_HEADS, N_KEYS, tt), lambda i: (0, 0, i))],
        out_specs=[out_spec, out_spec],
        out_shape=[jax.ShapeDtypeStruct((nt, tt, PEER_SLOTS), jnp.int32),
                   jax.ShapeDtypeStruct((nt, tt, PEER_SLOTS), F32)],
        scratch_shapes=[pltpu.VMEM((PEER_SLOTS, tt), jnp.int32), pltpu.VMEM((PEER_SLOTS, tt), F32),
                        pltpu.VMEM((PAIR_ROWS, tt), F32), pltpu.VMEM((PAIR_ROWS, tt), jnp.int32)],
        compiler_params=_cparams(("parallel",)),
        name="topk",
    )(scores_t)


def _for_each_token(idx_ref, idx_bufs, sems, token_fn, tt, group_fn=None):
    group = PEER_GROUP
    nbuf = len(idx_bufs)
    ngroups = tt // group
    ahead = nbuf - 1

    def ids_copy(g, k):
        return pltpu.make_async_copy(idx_ref.at[0, pl.ds(g * group, group)], idx_bufs[k], sems.at[k])

    for g in range(ahead):
        ids_copy(g, g).start()

    def round_(q, carry):
        for k in range(nbuf):
            g = q * nbuf + k
            ids_copy(g, k).wait()

            @pl.when(g + ahead < ngroups)
            def _():
                ids_copy(g + ahead, (k + ahead) % nbuf).start()

            if group_fn is not None:
                group_fn(g)
            for u in range(group):
                token_fn(g * group + u, u, idx_bufs[k].at[u])
        return carry

    lax.fori_loop(0, ngroups // nbuf, round_, 0)


def _id_scratch():
    return [pltpu.SMEM((PEER_GROUP, PEER_SLOTS), jnp.int32)] * PEER_ID_BUFS + [pltpu.SemaphoreType.DMA((PEER_ID_BUFS,))]


def _to_chunk_major(x_ref, chunk_sc, tt):
    for c in range(ROW_CHUNKS):
        chunk_sc[pl.ds(c, tt, stride=ROW_CHUNKS), :] = x_ref[:, c * LANES:(c + 1) * LANES]


def _sublane_sums(tiles):
    sub = lax.broadcasted_iota(jnp.int32, tiles[0].shape, 0)
    n = tiles[0].shape[0]
    bits = n.bit_length() - 1
    assert len(tiles) == n == 1 << bits

    def merge(a, b, step):
        first = (sub & step) == 0
        if 2 * step == n:
            return jnp.where(first, a, b) + pltpu.roll(jnp.where(first, b, a), step, 0)
        partner = jnp.where(first, pltpu.roll(a, n - step, 0), pltpu.roll(b, step, 0))
        return jnp.where(first, a, b) + partner

    level = [tiles[int(format(k, f"0{bits}b")[::-1], 2)] for k in range(n)]
    for lvl in range(bits):
        step = n >> (lvl + 1)
        level = [merge(level[i], level[i + 1], step) for i in range(0, len(level), 2)]
    return level[0]


def _peer_dot_kernel(idx_ref, x_ref, tab_ref, gate_ref, act_ref, *scratch, tt):
    idx_bufs, sems, x_sc, z_sc = scratch[:PEER_ID_BUFS], scratch[PEER_ID_BUFS], scratch[-2], scratch[-1]
    _to_chunk_major(x_ref, x_sc, tt)
    z_sc[...] = jnp.zeros(z_sc.shape, F32)
    lane = lax.broadcasted_iota(jnp.int32, z_sc.shape, 1)

    def token(t, u, slot_ids):
        xt = x_sc[pl.ds(pl.multiple_of(t * ROW_CHUNKS, ROW_CHUNKS), ROW_CHUNKS), :]
        cols = []
        for j0 in range(0, PEER_SLOTS, ROW_CHUNKS):
            prods = [tab_ref[slot_ids[j0 + k]].astype(F32) * xt for k in range(ROW_CHUNKS)]
            cols.append(jnp.sum(_sublane_sums(prods), axis=1, keepdims=True))
        col = jnp.concatenate(cols, axis=0)
        z_sc[...] = jnp.where(lane == t, col, z_sc[...])

    _for_each_token(idx_ref, idx_bufs, sems, token, tt)
    z = z_sc[...].T
    gelu = 0.5 * z * (1.0 + lax.erf(z * (1.0 / math.sqrt(2.0))))
    act_ref[0] = gelu * gate_ref[0]


def _peer_dot(idx_t, hn, tab, gate_t, tt):
    nt = idx_t.shape[0]
    d = hn.shape[1]
    tile = lambda i: (i, 0, 0)
    slots = pl.BlockSpec((1, tt, PEER_SLOTS), tile)
    return pl.pallas_call(
        functools.partial(_peer_dot_kernel, tt=tt),
        grid=(nt,),
        in_specs=[slots,
                  pl.BlockSpec((tt, d), lambda i: (i, 0)),
                  pl.BlockSpec(memory_space=pltpu.VMEM),
                  slots],
        out_specs=slots,
        out_shape=jax.ShapeDtypeStruct((nt, tt, PEER_SLOTS), F32),
        scratch_shapes=_id_scratch()
                       + [pltpu.VMEM((tt * ROW_CHUNKS, LANES), F32), pltpu.VMEM((PEER_SLOTS, tt), F32)],
        compiler_params=_cparams(("arbitrary",)),
        name="peer_dot",
    )(idx_t, hn, tab, gate_t)


def _peer_out_kernel(idx_ref, act_ref, tab_ref, h_ref, y_ref, *scratch, tt):
    idx_bufs, sems = scratch[:PEER_ID_BUFS], scratch[PEER_ID_BUFS]
    y_sc, actg_sc, w_sc = scratch[PEER_ID_BUFS + 1:]
    act_t = act_ref[0].T
    for g in range(tt // PEER_GROUP):
        actg_sc[g] = act_t[:, g * PEER_GROUP:(g + 1) * PEER_GROUP]
    nacc = 4

    def group_weights(g):
        act_g = actg_sc[g]
        for u in range(PEER_GROUP):
            w_sc[u * PEER_SLOTS:(u + 1) * PEER_SLOTS, :] = jnp.broadcast_to(act_g[:, u:u + 1], (PEER_SLOTS, LANES))

    def token(t, u, slot_ids):
        acc = [None] * nacc
        for j in range(PEER_SLOTS):
            row = tab_ref[slot_ids[j]].astype(F32)
            r = u * PEER_SLOTS + j
            term = jnp.broadcast_to(w_sc[r:r + 1, :], row.shape) * row
            acc[j % nacc] = term if acc[j % nacc] is None else acc[j % nacc] + term
        y_sc[pl.ds(pl.multiple_of(t * ROW_CHUNKS, ROW_CHUNKS), ROW_CHUNKS), :] = (acc[0] + acc[1]) + (acc[2] + acc[3])

    _for_each_token(idx_ref, idx_bufs, sems, token, tt, group_fn=group_weights)
    for c in range(ROW_CHUNKS):
        cols = slice(c * LANES, (c + 1) * LANES)
        y_ref[:, cols] = h_ref[:, cols] + y_sc[pl.ds(c, tt, stride=ROW_CHUNKS), :]


def _peer_out(idx_t, act_t, tab, h, tt):
    nt = idx_t.shape[0]
    tile = lambda i: (i, 0, 0)
    tok = pl.BlockSpec((tt, h.shape[1]), lambda i: (i, 0))
    return pl.pallas_call(
        functools.partial(_peer_out_kernel, tt=tt),
        grid=(nt,),
        in_specs=[pl.BlockSpec((1, tt, PEER_SLOTS), tile),
                  pl.BlockSpec((1, tt, PEER_SLOTS), tile),
                  pl.BlockSpec(memory_space=pltpu.VMEM), tok],
        out_specs=tok,
        out_shape=jax.ShapeDtypeStruct(h.shape, F32),
        scratch_shapes=_id_scratch()
                       + [pltpu.VMEM((tt * ROW_CHUNKS, LANES), F32),
                          pltpu.VMEM((tt // PEER_GROUP, PEER_SLOTS, PEER_GROUP), F32),
                          pltpu.VMEM((PEER_GROUP * PEER_SLOTS, LANES), F32)],
        compiler_params=_cparams(("arbitrary",)),
        name="peer_out",
    )(idx_t, act_t, tab, h)


def _chunk_table(tab):
    n, d = tab.shape
    return tab.astype(BF16).reshape(n, d // LANES, LANES)


def _rope_tables(seq):
    pos = jnp.arange(seq, dtype=F32)
    inv_freq = ROPE_THETA ** (-jnp.arange(0, ROT_DIM, 2, dtype=F32) / ROT_DIM)
    ang = pos[:, None] * inv_freq[None, :]
    cos, sin = jnp.cos(ang), jnp.sin(ang)
    half = ROT_DIM // 2
    pad1 = jnp.ones((seq, HEAD_DIM - ROT_DIM), F32)
    pad0 = jnp.zeros((seq, HEAD_DIM - ROT_DIM), F32)
    zeros = jnp.zeros((seq, half), F32)
    cos_h = jnp.concatenate([cos, cos, pad1], axis=1)
    sa_h = jnp.concatenate([-sin, zeros, pad0], axis=1)
    sb_h = jnp.concatenate([zeros, sin, pad0], axis=1)
    two = lambda a: jnp.concatenate([a, a], axis=1)
    return two(cos_h), two(sa_h), two(sb_h)


def _layer(x2, batch, seq, attn_norm_g, w_in, q_norm_a, k_norm_a, out_norm_a, q_norm_b, k_norm_b,
           lambda_q1, lambda_k1, lambda_q2, lambda_k2, out_norm_b, w_out, ffn_norm_g, w_peer_q,
           peer_sub_keys, peer_u, peer_v, peer_tt=128):
    tile_heads = lambda g: jnp.tile(g.astype(F32), WIDTH_A // HEAD_DIM)[None, :]
    gains = jnp.concatenate([tile_heads(q_norm_a), tile_heads(k_norm_a),
                             tile_heads(q_norm_b), tile_heads(k_norm_b)], axis=0)
    grp = jnp.arange(WIDTH_A) // HEAD_DIM
    bd = (grp[:, None] == grp[None, :]).astype(BF16)
    cos_t, sa_t, sb_t = _rope_tables(seq)

    qa, ka, va, qbt, kb, vbt = _inproj(x2, attn_norm_g[None, :].astype(F32), w_in.astype(BF16), gains, bd,
                                       cos_t, sa_t, sb_t, batch, seq)
    outs = [_dilated(q, k, v) for q, k, v in zip(qa, ka, va)]
    lams = [v[None, :].astype(F32) for v in (lambda_q1, lambda_k1, lambda_q2, lambda_k2)]
    ob = _diffattn(qbt, kb, vbt, lams, out_norm_b[None, :].astype(F32))

    w_out_bf = w_out.astype(BF16)
    sk = peer_sub_keys.reshape(2 * PEER_HEADS, N_KEYS, KEY_DIM).astype(BF16)
    h, hn, scores_t = _outproj(x2, [o for o, _ in outs] + [l for _, l in outs], ob,
                               w_out_bf[:WIDTH_A], w_out_bf[WIDTH_A:], out_norm_a[None, :].astype(F32), bd,
                               ffn_norm_g[None, :].astype(F32), w_peer_q.astype(BF16), sk)
    idx_t, gate_t = _topk(scores_t, peer_tt)
    act_t = _peer_dot(idx_t, hn, _chunk_table(peer_u), gate_t, peer_tt)
    return _peer_out(idx_t, act_t, _chunk_table(peer_v), h, peer_tt)


def kernel(x, attn_norm_g, w_in, q_norm_a, k_norm_a, out_norm_a, q_norm_b, k_norm_b, lambda_q1, lambda_k1,
           lambda_q2, lambda_k2, out_norm_b, w_out, ffn_norm_g, w_peer_q, peer_sub_keys, peer_u, peer_v):
    batch, seq, d = x.shape
    assert seq % (max(dil for _, dil in DILATED_PATTERNS) * BLOCK) == 0 and d == ROW_CHUNKS * LANES
    assert attn_norm_g.shape[0] == 1 and peer_u.shape[1:] == (N_KEYS * N_KEYS, d)
    y = _layer(x.reshape(batch * seq, d), batch, seq, attn_norm_g[0], w_in[0], q_norm_a[0], k_norm_a[0],
               out_norm_a[0], q_norm_b[0], k_norm_b[0], lambda_q1[0], lambda_k1[0], lambda_q2[0],
               lambda_k2[0], out_norm_b[0], w_out[0], ffn_norm_g[0], w_peer_q[0], peer_sub_keys[0],
               peer_u[0], peer_v[0])
    return y.reshape(batch, seq, d)
```

```python
import functools
import math

import jax
import jax.numpy as jnp
from jax import lax
from jax.experimental import pallas as pl
from jax.experimental.pallas import tpu as pltpu

F32 = jnp.float32
BF16 = jnp.bfloat16

HEAD_DIM = 64
N_HEADS_A = 8
WIDTH_A = N_HEADS_A * HEAD_DIM
DILATED_PATTERNS = ((128, 1), (512, 4), (2048, 16))
N_HEADS_B = 4
DIFF_DIM = HEAD_DIM
DIFF_V_DIM = 2 * DIFF_DIM
WIDTH_B = N_HEADS_B * DIFF_V_DIM
ROPE_THETA = 500000.0
ROT_DIM = HEAD_DIM // 4
BLOCK = 128
NORM_EPS = 1e-6
ATTN_SCALE = 1.0 / math.sqrt(HEAD_DIM)
LAM_INIT = 0.8 - 0.6 * math.exp(-0.3 * 0)

PEER_HEADS = 8
N_KEYS = 128
KEY_DIM = 128
PEER_TOPK = 16
PEER_SLOTS = PEER_HEADS * PEER_TOPK
ROW_CHUNKS = 8
PEER_GROUP = 8
PEER_ID_BUFS = 4

LANES = 128
VMEM_LIMIT = 56 * 1024 * 1024

NEG_INF = float("-inf")


def _cparams(sem):
    return pltpu.CompilerParams(dimension_semantics=sem, vmem_limit_bytes=VMEM_LIMIT)


def _split_bf16(a):
    hi = a.astype(BF16)
    lo = (a - hi.astype(F32)).astype(BF16)
    return hi, lo


def _group_sum(sq, bd):
    hi, lo = _split_bf16(sq)
    return (jnp.dot(hi, bd, preferred_element_type=F32)
            + jnp.dot(lo, bd, preferred_element_type=F32))


def _inproj_body(x_ref, g_ref, w_ref, gains_ref, bd_ref, cos_ref, sa_ref, sb_ref,
                 qa_refs, ka_refs, va_refs, qbt_ref, kb_ref, vbt_ref, stream_sc):
    x = x_ref[...]
    ms = jnp.mean(x * x, axis=-1, keepdims=True)
    xn = (x * lax.rsqrt(ms + NORM_EPS) * g_ref[...]).astype(BF16)
    proj = jnp.dot(xn, w_ref[...], preferred_element_type=F32)
    bd = bd_ref[...]
    rep = WIDTH_A // LANES
    cos = jnp.concatenate([cos_ref[...]] * rep, axis=1)
    sa = jnp.concatenate([sa_ref[...]] * rep, axis=1)
    sb = jnp.concatenate([sb_ref[...]] * rep, axis=1)
    half = ROT_DIM // 2

    def norm_rope(z, gain, scale):
        ssum = _group_sum(z * z, bd)
        zn = z * lax.rsqrt(ssum * (1.0 / HEAD_DIM) + NORM_EPS) * gain
        zr = (zn * cos + pltpu.roll(zn, WIDTH_A - half, 1) * sa + pltpu.roll(zn, half, 1) * sb)
        return zr * scale

    w = WIDTH_A
    tm = x.shape[0]
    mixer_a = (norm_rope(proj[:, 0:w], gains_ref[0:1, :], ATTN_SCALE),
               norm_rope(proj[:, w:2 * w], gains_ref[1:2, :], 1.0),
               proj[:, 2 * w:3 * w])
    for z, refs, sc in zip(mixer_a, (qa_refs, ka_refs, va_refs), stream_sc):
        nslab = w // LANES
        for k in range(nslab):
            sc[k] = z[:, k * LANES:(k + 1) * LANES]
        for (_, dil), ref in zip(DILATED_PATTERNS, refs):
            for r in range(dil):
                ref[r] = jnp.concatenate([sc[k, pl.ds(r, tm // dil, stride=dil), :] for k in range(nslab)],
                                         axis=1).astype(BF16)
    qbt_ref[...] = norm_rope(proj[:, 3 * w:4 * w], gains_ref[2:3, :], ATTN_SCALE).T.astype(BF16)
    kb_ref[...] = norm_rope(proj[:, 4 * w:5 * w], gains_ref[3:4, :], 1.0).astype(BF16)
    vbt_ref[...] = proj[:, 5 * w:6 * w].T.astype(BF16)


def _inproj_kernel(x_ref, g_ref, w_ref, gains_ref, bd_ref, cos_ref, sa_ref, sb_ref, *rest):
    nb = len(DILATED_PATTERNS)
    qa_refs, ka_refs, va_refs = rest[0:nb], rest[nb:2 * nb], rest[2 * nb:3 * nb]
    qbt_ref, kb_ref, vbt_ref = rest[3 * nb:3 * nb + 3]
    _inproj_body(x_ref, g_ref, w_ref, gains_ref, bd_ref, cos_ref, sa_ref, sb_ref,
                 qa_refs, ka_refs, va_refs, qbt_ref, kb_ref, vbt_ref, rest[3 * nb + 3:])


def _inproj(x2, g, w_bf, gains, bd, cos_t, sa_t, sb_t, batch, seq, tm=512):
    t, d = x2.shape
    ncol = w_bf.shape[1]
    nseq = seq // tm
    row = lambda i: (i, 0)
    fixed = lambda i: (0, 0)
    pos = lambda i: (i % nseq, 0)
    stream_specs, stream_sds = [], []
    for _, dil in DILATED_PATTERNS:
        stream_specs.append(pl.BlockSpec((None, dil, tm // dil, WIDTH_A), lambda i: (i // nseq, 0, i % nseq, 0)))
        stream_sds.append(jax.ShapeDtypeStruct((batch, dil, seq // dil, WIDTH_A), BF16))
    head_spec = pl.BlockSpec((None, WIDTH_B, tm), lambda i: (i // nseq, 0, i % nseq))
    head_sds = jax.ShapeDtypeStruct((batch, WIDTH_B, seq), BF16)
    outs = pl.pallas_call(
        _inproj_kernel,
        grid=(t // tm,),
        in_specs=[pl.BlockSpec((tm, d), row), pl.BlockSpec((1, d), fixed),
                  pl.BlockSpec((d, ncol), fixed), pl.BlockSpec((4, WIDTH_A), fixed),
                  pl.BlockSpec((WIDTH_A, WIDTH_A), fixed),
                  pl.BlockSpec((tm, LANES), pos), pl.BlockSpec((tm, LANES), pos),
                  pl.BlockSpec((tm, LANES), pos)],
        out_specs=stream_specs * 3 + [head_spec, pl.BlockSpec((tm, WIDTH_B), row), head_spec],
        out_shape=stream_sds * 3 + [head_sds, jax.ShapeDtypeStruct((t, WIDTH_B), BF16), head_sds],
        scratch_shapes=[pltpu.VMEM((WIDTH_A // LANES, tm, LANES), F32)] * 3,
        compiler_params=_cparams(("parallel",)),
        name="inproj",
    )(x2, g, w_bf, gains, bd, cos_t, sa_t, sb_t)
    nb = len(DILATED_PATTERNS)
    return outs[0:nb], outs[nb:2 * nb], outs[2 * nb:3 * nb], outs[3 * nb], outs[3 * nb + 1], outs[3 * nb + 2]


def _dilated_kernel(q_ref, kp_ref, k_ref, vp_ref, v_ref, o_ref, l_ref, kbuf, vbuf, *, tq):
    tile = pl.program_id(2)
    kbuf[0:BLOCK, :] = kp_ref[...]
    kbuf[BLOCK:, :] = k_ref[...]
    vbuf[0:BLOCK, :] = vp_ref[...]
    vbuf[BLOCK:, :] = v_ref[...]

    lane = lax.broadcasted_iota(jnp.int32, (BLOCK, LANES), 1)
    even = lane < HEAD_DIM
    head_masks = (even, lane >= HEAD_DIM)
    qi = lax.broadcasted_iota(jnp.int32, (BLOCK, 2 * BLOCK), 0)
    kj = lax.broadcasted_iota(jnp.int32, (BLOCK, 2 * BLOCK), 1)
    dist = qi + BLOCK - kj
    band = (dist >= 0) & (dist <= BLOCK)

    def one_block(blk):
        r0 = pl.multiple_of(blk * BLOCK, BLOCK)
        first_key = jnp.where((tile == 0) & (blk == 0), BLOCK, 0)
        valid = band & (kj >= first_key)
        for pair in range(WIDTH_A // LANES):
            cols = slice(pair * LANES, (pair + 1) * LANES)
            q = q_ref[pl.ds(r0, BLOCK), cols]
            kk = kbuf[pl.ds(r0, 2 * BLOCK), cols]
            vv = vbuf[pl.ds(r0, 2 * BLOCK), cols]
            outs, lses = [], []
            for head_mask in head_masks:
                qh = jnp.where(head_mask, q, jnp.zeros_like(q))
                s = lax.dot_general(qh, kk, (((1,), (1,)), ((), ())), preferred_element_type=F32)
                s = jnp.where(valid, s, NEG_INF)
                m = jnp.max(s, axis=-1, keepdims=True)
                p = jnp.exp(s - m)
                den = jnp.sum(p, axis=-1, keepdims=True)
                o = jnp.dot(p.astype(BF16), vv, preferred_element_type=F32) / den
                outs.append(o)
                lses.append(jnp.broadcast_to(m + jnp.log(den), (BLOCK, LANES)))
            o_ref[pl.ds(r0, BLOCK), cols] = jnp.where(even, outs[0], outs[1])
            l_ref[pl.ds(r0, BLOCK), cols] = jnp.where(even, lses[0], lses[1])

    def body(pair, carry):
        one_block(2 * pair)
        one_block(2 * pair + 1)
        return carry

    lax.fori_loop(0, tq // (2 * BLOCK), body, 0)


def _dilated(qa, ka, va):
    batch, dilation, stream, _ = qa.shape
    tq = min(stream, 1024)
    nprev = tq // BLOCK
    cur = lambda b, r, i: (b, r, i, 0)
    prev = lambda b, r, i: (b, r, jnp.maximum(i * nprev - 1, 0), 0)
    blk_cur = pl.BlockSpec((None, None, tq, WIDTH_A), cur)
    blk_prev = pl.BlockSpec((None, None, BLOCK, WIDTH_A), prev)
    out_sds = jax.ShapeDtypeStruct(qa.shape, F32)
    return pl.pallas_call(
        functools.partial(_dilated_kernel, tq=tq),
        grid=(batch, dilation, stream // tq),
        in_specs=[blk_cur, blk_prev, blk_cur, blk_prev, blk_cur],
        out_specs=[blk_cur, blk_cur],
        out_shape=[out_sds, out_sds],
        scratch_shapes=[pltpu.VMEM((tq + BLOCK, WIDTH_A), BF16), pltpu.VMEM((tq + BLOCK, WIDTH_A), BF16)],
        compiler_params=_cparams(("parallel", "parallel", "arbitrary")),
        name=f"dilated_d{dilation}",
    )(qa, ka, ka, va, va)


def _diff_kernel(lq1_ref, lk1_ref, lq2_ref, lk2_ref, gain_ref, qt_ref, k_ref, vt_ref, o_ref,
                 m_sc, l_sc, acc_sc, *, tq):
    i = pl.program_id(2)
    heads = qt_ref.shape[0] // DIFF_V_DIM
    dim = lax.broadcasted_iota(jnp.int32, (DIFF_V_DIM, tq), 0)
    qs = []
    for hh in range(heads):
        qt = qt_ref[hh * DIFF_V_DIM:(hh + 1) * DIFF_V_DIM, :]
        zero = jnp.zeros_like(qt)
        qs += [jnp.where(dim < DIFF_DIM, qt, zero), jnp.where(dim >= DIFF_DIM, qt, zero)]

    m_sc[...] = jnp.full(m_sc.shape, NEG_INF, F32)
    l_sc[...] = jnp.zeros(l_sc.shape, F32)
    acc_sc[...] = jnp.zeros(acc_sc.shape, F32)

    key = lax.broadcasted_iota(jnp.int32, (tq, tq), 0)
    query = lax.broadcasted_iota(jnp.int32, (tq, tq), 1)
    causal = key <= query

    def step(j, masked):
        r0 = pl.multiple_of(j * tq, tq)
        for hh in range(heads):
            cols = slice(hh * DIFF_V_DIM, (hh + 1) * DIFF_V_DIM)
            kb = k_ref[pl.ds(r0, tq), cols]
            vtb = vt_ref[cols, pl.ds(r0, tq)]
            for c in range(2 * hh, 2 * hh + 2):
                s = jnp.dot(kb, qs[c], preferred_element_type=F32)
                if masked:
                    s = jnp.where(causal, s, NEG_INF)
                m_old = m_sc[c]
                m_new = jnp.maximum(m_old, jnp.max(s, axis=0, keepdims=True))
                alpha = jnp.exp(m_old - m_new)
                p = jnp.exp(s - m_new)
                l_sc[c] = alpha * l_sc[c] + jnp.sum(p, axis=0, keepdims=True)
                acc_sc[c] = alpha * acc_sc[c] + jnp.dot(vtb, p.astype(BF16), preferred_element_type=F32)
                m_sc[c] = m_new

    def body(j, carry):
        step(j, False)
        return carry

    lax.fori_loop(0, i, body, 0)
    step(i, True)

    lam = (jnp.exp(jnp.sum(lq1_ref[...] * lk1_ref[...], keepdims=True))
           - jnp.exp(jnp.sum(lq2_ref[...] * lk2_ref[...], keepdims=True)) + LAM_INIT)
    for hh in range(heads):
        cols = slice(hh * DIFF_V_DIM, (hh + 1) * DIFF_V_DIM)
        ot = acc_sc[2 * hh] / l_sc[2 * hh] - lam * (acc_sc[2 * hh + 1] / l_sc[2 * hh + 1])
        o = ot.T
        ms = jnp.mean(o * o, axis=-1, keepdims=True)
        o = o * lax.rsqrt(ms + NORM_EPS) * gain_ref[:, cols] * (1.0 - LAM_INIT)
        o_ref[:, cols] = o.astype(o_ref.dtype)


def _diffattn(qbt, kb, vbt, lams, gain_b, tq=512, heads=2):
    batch, _, seq = qbt.shape
    t = batch * seq
    nq = seq // tq
    width = heads * DIFF_V_DIM
    lam_spec = pl.BlockSpec((1, DIFF_DIM), lambda b, h, i: (0, 0))
    k_spec = pl.BlockSpec((seq, width), lambda b, h, i: (b, h))
    vt_spec = pl.BlockSpec((None, width, seq), lambda b, h, i: (b, h, 0))
    qt_spec = pl.BlockSpec((None, width, tq), lambda b, h, i: (b, h, i))
    o_spec = pl.BlockSpec((tq, width), lambda b, h, i: (b * nq + i, h))
    return pl.pallas_call(
        functools.partial(_diff_kernel, tq=tq),
        grid=(batch, N_HEADS_B // heads, nq),
        in_specs=[lam_spec] * 4 + [pl.BlockSpec((1, width), lambda b, h, i: (0, h)), qt_spec, k_spec, vt_spec],
        out_specs=o_spec,
        out_shape=jax.ShapeDtypeStruct((t, WIDTH_B), BF16),
        scratch_shapes=[pltpu.VMEM((2 * heads, 1, tq), F32), pltpu.VMEM((2 * heads, 1, tq), F32),
                        pltpu.VMEM((2 * heads, DIFF_V_DIM, tq), F32)],
        compiler_params=_cparams(("parallel", "parallel", "arbitrary")),
        name="diffattn",
    )(*lams, gain_b, qbt, kb, vbt)


def _outproj_kernel(x_ref, o1_ref, o2_ref, o3_ref, l1_ref, l2_ref, l3_ref, ob_ref, wa_ref, wb_ref,
                    gna_ref, bd_ref, gffn_ref, wq_ref, sk_ref, h_ref, hn_ref, sc_ref, *order_sc):
    tm = x_ref.shape[0]

    def token_order(ref, sc):
        dil = ref.shape[0]
        if dil == 1:
            return ref[0]
        nslab = WIDTH_A // LANES
        for r in range(dil):
            for k in range(nslab):
                sc[k, pl.ds(r, tm // dil, stride=dil), :] = ref[r, :, k * LANES:(k + 1) * LANES]
        return jnp.concatenate([sc[k] for k in range(nslab)], axis=1)

    o1, o2, o3, l1, l2, l3 = [token_order(ref, sc) for ref, sc in
                              zip((o1_ref, o2_ref, o3_ref, l1_ref, l2_ref, l3_ref), order_sc)]
    lmax = jnp.maximum(jnp.maximum(l1, l2), l3)
    e1, e2, e3 = jnp.exp(l1 - lmax), jnp.exp(l2 - lmax), jnp.exp(l3 - lmax)
    oa = (e1 * o1 + e2 * o2 + e3 * o3) / (e1 + e2 + e3)
    ssum = _group_sum(oa * oa, bd_ref[...])
    oa = oa * lax.rsqrt(ssum * (1.0 / HEAD_DIM) + NORM_EPS) * gna_ref[...]
    h = (x_ref[...]
         + jnp.dot(oa.astype(BF16), wa_ref[...], preferred_element_type=F32)
         + jnp.dot(ob_ref[...], wb_ref[...], preferred_element_type=F32))
    h_ref[...] = h
    ms = jnp.mean(h * h, axis=-1, keepdims=True)
    hn = h * lax.rsqrt(ms + NORM_EPS) * gffn_ref[...]
    hn_ref[...] = hn
    q = jnp.dot(hn.astype(BF16), wq_ref[...], preferred_element_type=F32).astype(BF16)
    for hc in range(2 * PEER_HEADS):
        qs = q[:, hc * KEY_DIM:(hc + 1) * KEY_DIM]
        sc_ref[hc] = lax.dot_general(sk_ref[hc], qs, (((1,), (1,)), ((), ())), preferred_element_type=F32)


def _outproj(x2, o_l, ob, wa, wb, gna, bd, gffn, wq, sk, tm=256):
    t, d = x2.shape
    nseq = (o_l[0].shape[1] * o_l[0].shape[2]) // tm
    row = lambda i: (i, 0)
    fixed = lambda i: (0, 0)
    half = pl.BlockSpec((tm, WIDTH_A), row)
    full = pl.BlockSpec((tm, d), row)
    streams = [pl.BlockSpec((None, a.shape[1], tm // a.shape[1], WIDTH_A),
                            lambda i: (i // nseq, 0, i % nseq, 0)) for a in o_l]
    nq = wq.shape[1]
    return pl.pallas_call(
        _outproj_kernel,
        grid=(t // tm,),
        in_specs=[full] + streams + [half]
                 + [pl.BlockSpec((WIDTH_A, d), fixed), pl.BlockSpec((WIDTH_B, d), fixed),
                    pl.BlockSpec((1, WIDTH_A), fixed), pl.BlockSpec((WIDTH_A, WIDTH_A), fixed),
                    pl.BlockSpec((1, d), fixed), pl.BlockSpec((d, nq), fixed),
                    pl.BlockSpec((2 * PEER_HEADS, N_KEYS, KEY_DIM), lambda i: (0, 0, 0))],
        out_specs=[full, full, pl.BlockSpec((2 * PEER_HEADS, N_KEYS, tm), lambda i: (0, 0, i))],
        out_shape=[jax.ShapeDtypeStruct((t, d), F32), jax.ShapeDtypeStruct((t, d), F32),
                   jax.ShapeDtypeStruct((2 * PEER_HEADS, N_KEYS, t), F32)],
        scratch_shapes=[pltpu.VMEM((WIDTH_A // LANES, tm, LANES), F32)] * len(o_l),
        compiler_params=_cparams(("parallel",)),
        name="outproj",
    )(x2, *o_l, ob, wa, wb, gna, bd, gffn, wq, sk)


def _extract_top(vals, payload, count):
    rows = vals.shape[0]
    iota = lax.broadcasted_iota(jnp.int32, vals.shape, 0).astype(F32)
    tops, picks = [], []
    for _ in range(count):
        m = jnp.max(vals, axis=0, keepdims=True)
        win = jnp.min(jnp.where(vals == m, iota, float(rows)), axis=0, keepdims=True)
        hit = iota == win
        tops.append(m)
        picks.append(win.astype(jnp.int32) if payload is None
                     else jnp.sum(jnp.where(hit, payload, 0), axis=0, keepdims=True))
        vals = jnp.where(hit, NEG_INF, vals)
    return tops, picks


PAIR_COUNTS = tuple(PEER_TOPK // (a + 1) for a in range(PEER_TOPK))
N_PAIRS = sum(PAIR_COUNTS)
PAIR_ROWS = -(-N_PAIRS // 8) * 8


def _topk_kernel(sc_ref, idx_ref, gate_ref, idx_sc, gate_sc, cand_sc, cid_sc):
    tokens = cand_sc.shape[1]
    cand_sc[N_PAIRS:, :] = jnp.full((PAIR_ROWS - N_PAIRS, tokens), NEG_INF, F32)
    cid_sc[N_PAIRS:, :] = jnp.zeros((PAIR_ROWS - N_PAIRS, tokens), jnp.int32)
    for h in range(PEER_HEADS):
        v1, i1 = _extract_top(sc_ref[2 * h], None, PEER_TOPK)
        v2, i2 = _extract_top(sc_ref[2 * h + 1], None, PEER_TOPK)
        v2m = jnp.concatenate(v2, axis=0)
        i2m = jnp.concatenate(i2, axis=0)
        off = 0
        for a, nb in enumerate(PAIR_COUNTS):
            cand_sc[off:off + nb, :] = v1[a] + v2m[0:nb]
            cid_sc[off:off + nb, :] = i1[a] * N_KEYS + i2m[0:nb]
            off += nb
        top, eid = _extract_top(cand_sc[...], cid_sc[...], PEER_TOPK)
        ex = [jnp.exp(tv - top[0]) for tv in top]
        den = ex[0]
        for e in ex[1:]:
            den = den + e
        for k in range(PEER_TOPK):
            j = h * PEER_TOPK + k
            idx_sc[j:j + 1, :] = eid[k]
            gate_sc[j:j + 1, :] = ex[k] / den
    idx_ref[0] = idx_sc[...].T
    gate_ref[0] = gate_sc[...].T


def _topk(scores_t, tt):
    t = scores_t.shape[-1]
    nt = t // tt
    out_spec = pl.BlockSpec((1, tt, PEER_SLOTS), lambda i: (i, 0, 0))
    return pl.pallas_call(
        _topk_kernel,
        grid=(nt,),
        in_specs=[pl.BlockSpec((2 * PEER_HEADS, N_KEYS, tt), lambda i: (0, 0, i))],
        out_specs=[out_spec, out_spec],
        out_shape=[jax.ShapeDtypeStruct((nt, tt, PEER_SLOTS), jnp.int32),
                   jax.ShapeDtypeStruct((nt, tt, PEER_SLOTS), F32)],
        scratch_shapes=[pltpu.VMEM((PEER_SLOTS, tt), jnp.int32), pltpu.VMEM((PEER_SLOTS, tt), F32),
                        pltpu.VMEM((PAIR_ROWS, tt), F32), pltpu.VMEM((PAIR_ROWS, tt), jnp.int32)],
        compiler_params=_cparams(("parallel",)),
        name="topk",
    )(scores_t)


def _for_each_token(idx_ref, idx_bufs, sems, token_fn, tt, group_fn=None):
    group = PEER_GROUP
    nbuf = len(idx_bufs)
    ngroups = tt // group
    ahead = nbuf - 1

    def ids_copy(g, k):
        return pltpu.make_async_copy(idx_ref.at[0, pl.ds(g * group, group)], idx_bufs[k], sems.at[k])

    for g in range(ahead):
        ids_copy(g, g).start()

    def round_(q, carry):
        for k in range(nbuf):
            g = q * nbuf + k
            ids_copy(g, k).wait()

            @pl.when(g + ahead < ngroups)
            def _():
                ids_copy(g + ahead, (k + ahead) % nbuf).start()

            if group_fn is not None:
                group_fn(g)
            for u in range(group):
                token_fn(g * group + u, u, idx_bufs[k].at[u])
        return carry

    lax.fori_loop(0, ngroups // nbuf, round_, 0)


def _id_scratch():
    return [pltpu.SMEM((PEER_GROUP, PEER_SLOTS), jnp.int32)] * PEER_ID_BUFS + [pltpu.SemaphoreType.DMA((PEER_ID_BUFS,))]


def _to_chunk_major(x_ref, chunk_sc, tt):
    for c in range(ROW_CHUNKS):
        chunk_sc[pl.ds(c, tt, stride=ROW_CHUNKS), :] = x_ref[:, c * LANES:(c + 1) * LANES]


def _sublane_sums(tiles):
    sub = lax.broadcasted_iota(jnp.int32, tiles[0].shape, 0)
    n = tiles[0].shape[0]
    bits = n.bit_length() - 1
    assert len(tiles) == n == 1 << bits

    def merge(a, b, step):
        first = (sub & step) == 0
        if 2 * step == n:
            return jnp.where(first, a, b) + pltpu.roll(jnp.where(first, b, a), step, 0)
        partner = jnp.where(first, pltpu.roll(a, n - step, 0), pltpu.roll(b, step, 0))
        return jnp.where(first, a, b) + partner

    level = [tiles[int(format(k, f"0{bits}b")[::-1], 2)] for k in range(n)]
    for lvl in range(bits):
        step = n >> (lvl + 1)
        level = [merge(level[i], level[i + 1], step) for i in range(0, len(level), 2)]
    return level[0]


def _peer_dot_kernel(idx_ref, x_ref, tab_ref, gate_ref, act_ref, *scratch, tt):
    idx_bufs, sems, x_sc, z_sc = scratch[:PEER_ID_BUFS], scratch[PEER_ID_BUFS], scratch[-2], scratch[-1]
    _to_chunk_major(x_ref, x_sc, tt)
    z_sc[...] = jnp.zeros(z_sc.shape, F32)
    lane = lax.broadcasted_iota(jnp.int32, z_sc.shape, 1)

    def token(t, u, slot_ids):
        xt = x_sc[pl.ds(pl.multiple_of(t * ROW_CHUNKS, ROW_CHUNKS), ROW_CHUNKS), :]
        cols = []
        for j0 in range(0, PEER_SLOTS, ROW_CHUNKS):
            prods = [tab_ref[slot_ids[j0 + k]].astype(F32) * xt for k in range(ROW_CHUNKS)]
            cols.append(jnp.sum(_sublane_sums(prods), axis=1, keepdims=True))
        col = jnp.concatenate(cols, axis=0)
        z_sc[...] = jnp.where(lane == t, col, z_sc[...])

    _for_each_token(idx_ref, idx_bufs, sems, token, tt)
    z = z_sc[...].T
    gelu = 0.5 * z * (1.0 + lax.erf(z * (1.0 / math.sqrt(2.0))))
    act_ref[0] = gelu * gate_ref[0]


def _peer_dot(idx_t, hn, tab, gate_t, tt):
    nt = idx_t.shape[0]
    d = hn.shape[1]
    tile = lambda i: (i, 0, 0)
    slots = pl.BlockSpec((1, tt, PEER_SLOTS), tile)
    return pl.pallas_call(
        functools.partial(_peer_dot_kernel, tt=tt),
        grid=(nt,),
        in_specs=[slots,
                  pl.BlockSpec((tt, d), lambda i: (i, 0)),
                  pl.BlockSpec(memory_space=pltpu.VMEM),
                  slots],
        out_specs=slots,
        out_shape=jax.ShapeDtypeStruct((nt, tt, PEER_SLOTS), F32),
        scratch_shapes=_id_scratch()
                       + [pltpu.VMEM((tt * ROW_CHUNKS, LANES), F32), pltpu.VMEM((PEER_SLOTS, tt), F32)],
        compiler_params=_cparams(("arbitrary",)),
        name="peer_dot",
    )(idx_t, hn, tab, gate_t)


def _peer_out_kernel(idx_ref, act_ref, tab_ref, h_ref, y_ref, *scratch, tt):
    idx_bufs, sems = scratch[:PEER_ID_BUFS], scratch[PEER_ID_BUFS]
    y_sc, actg_sc, w_sc = scratch[PEER_ID_BUFS + 1:]
    act_t = act_ref[0].T
    for g in range(tt // PEER_GROUP):
        actg_sc[g] = act_t[:, g * PEER_GROUP:(g + 1) * PEER_GROUP]
    nacc = 4

    def group_weights(g):
        act_g = actg_sc[g]
        for u in range(PEER_GROUP):
            w_sc[u * PEER_SLOTS:(u + 1) * PEER_SLOTS, :] = jnp.broadcast_to(act_g[:, u:u + 1], (PEER_SLOTS, LANES))

    def token(t, u, slot_ids):
        acc = [None] * nacc
        comp = [None] * nacc
        for j in range(PEER_SLOTS):
            row = tab_ref[slot_ids[j]].astype(F32)
            r = u * PEER_SLOTS + j
            term = jnp.broadcast_to(w_sc[r:r + 1, :], row.shape) * row
            k = j % nacc
            if acc[k] is None:
                acc[k], comp[k] = term, jnp.zeros_like(term)
            else:
                y = term - comp[k]
                s = acc[k] + y
                comp[k] = (s - acc[k]) - y
                acc[k] = s
        y_sc[pl.ds(pl.multiple_of(t * ROW_CHUNKS, ROW_CHUNKS), ROW_CHUNKS), :] = (acc[0] + acc[1]) + (acc[2] + acc[3])

    _for_each_token(idx_ref, idx_bufs, sems, token, tt, group_fn=group_weights)
    for c in range(ROW_CHUNKS):
        cols = slice(c * LANES, (c + 1) * LANES)
        y_ref[:, cols] = h_ref[:, cols] + y_sc[pl.ds(c, tt, stride=ROW_CHUNKS), :]


def _peer_out(idx_t, act_t, tab, h, tt):
    nt = idx_t.shape[0]
    tile = lambda i: (i, 0, 0)
    tok = pl.BlockSpec((tt, h.shape[1]), lambda i: (i, 0))
    return pl.pallas_call(
        functools.partial(_peer_out_kernel, tt=tt),
        grid=(nt,),
        in_specs=[pl.BlockSpec((1, tt, PEER_SLOTS), tile),
                  pl.BlockSpec((1, tt, PEER_SLOTS), tile),
                  pl.BlockSpec(memory_space=pltpu.VMEM), tok],
        out_specs=tok,
        out_shape=jax.ShapeDtypeStruct(h.shape, F32),
        scratch_shapes=_id_scratch()
                       + [pltpu.VMEM((tt * ROW_CHUNKS, LANES), F32),
                          pltpu.VMEM((tt // PEER_GROUP, PEER_SLOTS, PEER_GROUP), F32),
                          pltpu.VMEM((PEER_GROUP * PEER_SLOTS, LANES), F32)],
        compiler_params=_cparams(("arbitrary",)),
        name="peer_out",
    )(idx_t, act_t, tab, h)


def _chunk_table(tab):
    n, d = tab.shape
    return tab.astype(BF16).reshape(n, d // LANES, LANES)


def _rope_tables(seq):
    pos = jnp.arange(seq, dtype=F32)
    inv_freq = ROPE_THETA ** (-jnp.arange(0, ROT_DIM, 2, dtype=F32) / ROT_DIM)
    ang = pos[:, None] * inv_freq[None, :]
    cos, sin = jnp.cos(ang), jnp.sin(ang)
    half = ROT_DIM // 2
    pad1 = jnp.ones((seq, HEAD_DIM - ROT_DIM), F32)
    pad0 = jnp.zeros((seq, HEAD_DIM - ROT_DIM), F32)
    zeros = jnp.zeros((seq, half), F32)
    cos_h = jnp.concatenate([cos, cos, pad1], axis=1)
    sa_h = jnp.concatenate([-sin, zeros, pad0], axis=1)
    sb_h = jnp.concatenate([zeros, sin, pad0], axis=1)
    two = lambda a: jnp.concatenate([a, a], axis=1)
    return two(cos_h), two(sa_h), two(sb_h)


def _layer(x2, batch, seq, attn_norm_g, w_in, q_norm_a, k_norm_a, out_norm_a, q_norm_b, k_norm_b,
           lambda_q1, lambda_k1, lambda_q2, lambda_k2, out_norm_b, w_out, ffn_norm_g, w_peer_q,
           peer_sub_keys, peer_u, peer_v, peer_tt=128):
    tile_heads = lambda g: jnp.tile(g.astype(F32), WIDTH_A // HEAD_DIM)[None, :]
    gains = jnp.concatenate([tile_heads(q_norm_a), tile_heads(k_norm_a),
                             tile_heads(q_norm_b), tile_heads(k_norm_b)], axis=0)
    grp = jnp.arange(WIDTH_A) // HEAD_DIM
    bd = (grp[:, None] == grp[None, :]).astype(BF16)
    cos_t, sa_t, sb_t = _rope_tables(seq)

    qa, ka, va, qbt, kb, vbt = _inproj(x2, attn_norm_g[None, :].astype(F32), w_in.astype(BF16), gains, bd,
                                       cos_t, sa_t, sb_t, batch, seq)
    outs = [_dilated(q, k, v) for q, k, v in zip(qa, ka, va)]
    lams = [v[None, :].astype(F32) for v in (lambda_q1, lambda_k1, lambda_q2, lambda_k2)]
    ob = _diffattn(qbt, kb, vbt, lams, out_norm_b[None, :].astype(F32))

    w_out_bf = w_out.astype(BF16)
    sk = peer_sub_keys.reshape(2 * PEER_HEADS, N_KEYS, KEY_DIM).astype(BF16)
    h, hn, scores_t = _outproj(x2, [o for o, _ in outs] + [l for _, l in outs], ob,
                               w_out_bf[:WIDTH_A], w_out_bf[WIDTH_A:], out_norm_a[None, :].astype(F32), bd,
                               ffn_norm_g[None, :].astype(F32), w_peer_q.astype(BF16), sk)
    idx_t, gate_t = _topk(scores_t, peer_tt)
    act_t = _peer_dot(idx_t, hn, _chunk_table(peer_u), gate_t, peer_tt)
    return _peer_out(idx_t, act_t, _chunk_table(peer_v), h, peer_tt)


def kernel(x, attn_norm_g, w_in, q_norm_a, k_norm_a, out_norm_a, q_norm_b, k_norm_b, lambda_q1, lambda_k1,
           lambda_q2, lambda_k2, out_norm_b, w_out, ffn_norm_g, w_peer_q, peer_sub_keys, peer_u, peer_v):
    batch, seq, d = x.shape
    assert seq % (max(dil for _, dil in DILATED_PATTERNS) * BLOCK) == 0 and d == ROW_CHUNKS * LANES
    assert attn_norm_g.shape[0] == 1 and peer_u.shape[1:] == (N_KEYS * N_KEYS, d)
    y = _layer(x.reshape(batch * seq, d), batch, seq, attn_norm_g[0], w_in[0], q_norm_a[0], k_norm_a[0],
               out_norm_a[0], q_norm_b[0], k_norm_b[0], lambda_q1[0], lambda_k1[0], lambda_q2[0],
               lambda_k2[0], out_norm_b[0], w_out[0], ffn_norm_g[0], w_peer_q[0], peer_sub_keys[0],
               peer_u[0], peer_v[0])
    return y.reshape(batch, seq, d)
```
